```python
import math
import jax, jax.numpy as jnp
from jax import lax
import numpy as np

D_MODEL = 1024
BATCH = 8
SEQ = 4096
DEPTH = 2

CHUNK = 64
Q_BLOCK = 128
HEAD_DIM = 64
H_DIFF = 4
H_FOX = 6
H_CHUNK = 6
N_PREV_CHUNKS = 8
REL_CLIP = 128
D_FF = 2816
CONV_WIDTH = 3
PLE_DIM = 256
EPS = 1e-6
NEG_INF = -1e30

W_DIFF = H_DIFF * HEAD_DIM
W_FOX = H_FOX * HEAD_DIM
W_CHUNK = H_CHUNK * HEAD_DIM
MIX_WIDTH = W_DIFF + W_FOX + W_CHUNK
IN_SIZES = (W_DIFF,) * 5 + (W_FOX,) * 3 + (H_FOX,) + (W_CHUNK,) * 3
IN_COLS = sum(IN_SIZES)
IN_SPLITS = tuple(int(c) for c in np.cumsum(IN_SIZES)[:-1])
ALIBI_SLOPES = tuple(2.0 ** (-8.0 * (h + 1) / H_DIFF) for h in range(H_DIFF))

kernel_name = "hymba_style_chunk_causal_hybrid_block"


def rms_norm(x, g):
    xf = x.astype(jnp.float32)
    y = xf * lax.rsqrt(jnp.mean(xf * xf, axis=-1, keepdims=True) + EPS)
    return (y * g.astype(jnp.float32)).astype(x.dtype)


def split_heads(t, n_heads):
    return t.reshape(t.shape[0], t.shape[1], n_heads, HEAD_DIM)


def query_blocks(q):
    b, s = q.shape[0], q.shape[1]
    q = q.reshape((b, s // Q_BLOCK, Q_BLOCK) + q.shape[2:])
    return jnp.swapaxes(q, 0, 1)


def merge_blocks(o):
    nb, b, qb, h, d = o.shape
    return jnp.swapaxes(o, 0, 1).reshape(b, nb * qb, h * d)


def diff_attention(q1, q2, k1, k2, v, lam, subln_g, lam_init):
    S = k1.shape[1]
    slopes = jnp.asarray(ALIBI_SLOPES, jnp.float32)
    kpos = jnp.arange(S)
    scale = HEAD_DIM ** -0.5

    def block(args):
        q1b, q2b, blk = args
        qpos = blk * Q_BLOCK + jnp.arange(Q_BLOCK)
        dist = jnp.abs(qpos[:, None] - kpos[None, :]).astype(jnp.float32)
        bias = -slopes[:, None, None] * dist
        allowed = (kpos[None, :] // CHUNK) <= (qpos[:, None] // CHUNK)

        def probs(qb, k):
            s = jnp.einsum('bqhd,bkhd->bhqk', qb, k).astype(jnp.float32) * scale + bias
            s = jnp.where(allowed, s, NEG_INF)
            return jax.nn.softmax(s, axis=-1)

        a = probs(q1b, k1) - lam * probs(q2b, k2)
        return jnp.einsum('bhqk,bkhd->bqhd', a.astype(v.dtype), v)

    nb = S // Q_BLOCK
    o = lax.map(block, (query_blocks(q1), query_blocks(q2), jnp.arange(nb)))
    o = rms_norm(o, subln_g) * (1.0 - lam_init)
    return merge_blocks(o)


def forgetting_attention(q, k, v, log_f):
    B, S, H, _ = q.shape
    F = jnp.cumsum(log_f, axis=1)
    F_k = jnp.transpose(F, (0, 2, 1))
    F_q = jnp.transpose(query_blocks(F), (0, 1, 3, 2))
    kpos = jnp.arange(S)
    scale = HEAD_DIM ** -0.5

    def block(args):
        qb, fq, blk = args
        qpos = blk * Q_BLOCK + jnp.arange(Q_BLOCK)
        s = jnp.einsum('bqhd,bkhd->bhqk', qb, k).astype(jnp.float32) * scale
        s = s + fq[..., None] - F_k[:, :, None, :]
        allowed = kpos[None, :] <= qpos[:, None]
        s = jnp.where(allowed, s, NEG_INF)
        pr = jax.nn.softmax(s, axis=-1)
        return jnp.einsum('bhqk,bkhd->bqhd', pr.astype(v.dtype), v)

    nb = S // Q_BLOCK
    o = lax.map(block, (query_blocks(q), F_q, jnp.arange(nb)))
    return merge_blocks(o)


def chunk_band(t):
    b, s, h, d = t.shape
    nc = s // CHUNK
    tc = t.reshape(b, nc, CHUNK, h, d)
    tp = jnp.pad(tc, ((0, 0), (N_PREV_CHUNKS, 0), (0, 0), (0, 0), (0, 0)))
    return jnp.concatenate([tp[:, j:j + nc] for j in range(N_PREV_CHUNKS + 1)], axis=2)


def chunk_attention(q, k, v, rel_table):
    B, S, H, d = q.shape
    nc = S // CHUNK
    band_len = (N_PREV_CHUNKS + 1) * CHUNK
    qc = q.reshape(B, nc, CHUNK, H, d)
    kb = chunk_band(k)
    vb = chunk_band(v)
    qq = np.arange(CHUNK)[:, None]
    kk = np.arange(band_len)[None, :]
    rel = N_PREV_CHUNKS * CHUNK + qq - kk
    idx = np.clip(rel, -REL_CLIP, REL_CLIP) + REL_CLIP
    bias = rel_table[:, idx].astype(jnp.float32)
    key_chunk = (jnp.arange(nc)[:, None] - N_PREV_CHUNKS
                 + (jnp.arange(band_len) // CHUNK)[None, :])
    valid = (key_chunk >= 0)[:, None, None, :]
    s = jnp.einsum('bcqhd,bckhd->bchqk', qc, kb).astype(jnp.float32) * (HEAD_DIM ** -0.5) + bias
    s = jnp.where(valid, s, NEG_INF)
    pr = jax.nn.softmax(s, axis=-1)
    o = jnp.einsum('bchqk,bckhd->bcqhd', pr.astype(v.dtype), vb)
    return o.reshape(B, S, H * d)


def conv_gated_mlp(h, w_up, conv_w, conv_b, w_down):
    u = h @ w_up
    S = u.shape[1]
    up = jnp.pad(u, ((0, 0), (CONV_WIDTH - 1, 0), (0, 0)))
    c = conv_b
    for j in range(CONV_WIDTH):
        c = c + conv_w[j] * up[:, j:j + S]
    gate, val = jnp.split(c, 2, axis=-1)
    return (jax.nn.silu(gate) * val) @ w_down


def setup_inputs(seed: int = 0) -> dict:
    key = jax.random.key(seed)
    ks = jax.random.split(key, 20)
    f32 = jnp.float32
    nrm = lambda k, shape, s: jax.random.normal(k, shape, f32) * s
    gain = lambda k, shape: 1.0 + 0.05 * jax.random.normal(k, shape, f32)
    return {
        "x": jax.random.normal(ks[0], (BATCH, SEQ, D_MODEL), f32),
        "p": jax.random.normal(ks[1], (DEPTH, BATCH, SEQ, PLE_DIM), f32),
        "ln_mix": gain(ks[2], (DEPTH, D_MODEL)),
        "w_in": nrm(ks[3], (DEPTH, D_MODEL, IN_COLS), D_MODEL ** -0.5),
        "qk_gain": gain(ks[4], (DEPTH, 6, HEAD_DIM)),
        "lam_params": nrm(ks[5], (DEPTH, 4, HEAD_DIM), 0.1),
        "subln_gain": gain(ks[6], (DEPTH, HEAD_DIM)),
        "fgate_bias": jax.random.uniform(ks[7], (DEPTH, H_FOX), f32, 1.0, 4.0),
        "rel_bias": nrm(ks[8], (DEPTH, H_CHUNK, 2 * REL_CLIP + 1), 0.1),
        "w_out": nrm(ks[9], (DEPTH, MIX_WIDTH, D_MODEL), MIX_WIDTH ** -0.5),
        "ln_ffn": gain(ks[10], (DEPTH, D_MODEL)),
        "w_up": nrm(ks[11], (DEPTH, D_MODEL, 2 * D_FF), D_MODEL ** -0.5),
        "conv_w": nrm(ks[12], (DEPTH, CONV_WIDTH, 2 * D_FF), CONV_WIDTH ** -0.5),
        "conv_b": nrm(ks[13], (DEPTH, 2 * D_FF), 0.02),
        "w_down": nrm(ks[14], (DEPTH, D_FF, D_MODEL), D_FF ** -0.5),
        "ln_ple": gain(ks[15], (DEPTH, D_MODEL)),
        "w_ple_gate": nrm(ks[16], (DEPTH, D_MODEL, D_MODEL), D_MODEL ** -0.5),
        "w_ple_proj": nrm(ks[17], (DEPTH, PLE_DIM, D_MODEL), PLE_DIM ** -0.5),
    }


def reference(x, p, ln_mix, w_in, qk_gain, lam_params, subln_gain, fgate_bias, rel_bias,
              w_out, ln_ffn, w_up, conv_w, conv_b, w_down, ln_ple, w_ple_gate, w_ple_proj):
    h = x
    for i in range(DEPTH):
        hn = rms_norm(h, ln_mix[i])
        z = hn @ w_in[i]
        (q1, q2, k1, k2, va, qf, kf, vf, fg, qc, kc, vc) = jnp.split(z, IN_SPLITS, axis=-1)
        g = qk_gain[i]
        q1 = rms_norm(split_heads(q1, H_DIFF), g[0])
        q2 = rms_norm(split_heads(q2, H_DIFF), g[0])
        k1 = rms_norm(split_heads(k1, H_DIFF), g[1])
        k2 = rms_norm(split_heads(k2, H_DIFF), g[1])
        va = split_heads(va, H_DIFF)
        lp = lam_params[i].astype(jnp.float32)
        lam_init = 0.8 - 0.6 * math.exp(-0.3 * i)
        lam = jnp.exp(jnp.sum(lp[0] * lp[1])) - jnp.exp(jnp.sum(lp[2] * lp[3])) + lam_init
        o_a = diff_attention(q1, q2, k1, k2, va, lam, subln_gain[i], lam_init)
        qf = rms_norm(split_heads(qf, H_FOX), g[2])
        kf = rms_norm(split_heads(kf, H_FOX), g[3])
        vf = split_heads(vf, H_FOX)
        log_f = jax.nn.log_sigmoid(fg.astype(jnp.float32) + fgate_bias[i].astype(jnp.float32))
        o_b = forgetting_attention(qf, kf, vf, log_f)
        qc = rms_norm(split_heads(qc, H_CHUNK), g[4])
        kc = rms_norm(split_heads(kc, H_CHUNK), g[5])
        vc = split_heads(vc, H_CHUNK)
        o_c = chunk_attention(qc, kc, vc, rel_bias[i])
        h = h + jnp.concatenate([o_a, o_b, o_c], axis=-1) @ w_out[i]
        h = h + conv_gated_mlp(rms_norm(h, ln_ffn[i]), w_up[i], conv_w[i], conv_b[i], w_down[i])
        gate = jax.nn.sigmoid(rms_norm(h, ln_ple[i]) @ w_ple_gate[i])
        h = h + (p[i] @ w_ple_proj[i]) * gate
    return h
```

```python
import functools
import math

import numpy as np
import jax
import jax.numpy as jnp
from jax import lax
from jax.experimental import pallas as pl
from jax.experimental.pallas import tpu as pltpu

F32 = jnp.float32
BF16 = jnp.bfloat16

D_MODEL = 1024
CHUNK = 64
HEAD_DIM = 64
H_DIFF = 4
H_FOX = 6
H_CHUNK = 6
N_PREV_CHUNKS = 8
REL_CLIP = 128
D_FF = 2816
CONV_WIDTH = 3
PLE_DIM = 256
EPS = 1e-6
NEG_INF = -1e30

W_DIFF = H_DIFF * HEAD_DIM
W_FOX = H_FOX * HEAD_DIM
W_CHUNK = H_CHUNK * HEAD_DIM
MIX_WIDTH = W_DIFF + W_FOX + W_CHUNK
ALIBI_SLOPES = tuple(2.0 ** (-8.0 * (h + 1) / H_DIFF) for h in range(H_DIFF))

LANES = 128
HEADS_PER_BLOCK = LANES // HEAD_DIM
FG_ROWS = 8

_Z_SIZES = (W_DIFF,) * 4 + (W_FOX,) * 2 + (W_CHUNK,) * 2 + (W_DIFF, W_FOX, W_CHUNK)
_Z_OFFS = tuple(int(c) for c in np.cumsum((0,) + _Z_SIZES))
Z_COLS = _Z_OFFS[-1]
Z_NORM_COLS = _Z_OFFS[8]
(ZB_Q1, ZB_Q2, ZB_K1, ZB_K2, ZB_QF, ZB_KF, ZB_QC, ZB_KC, ZB_VA, ZB_VF, ZB_VC) = (
    o // LANES for o in _Z_OFFS[:-1])

VMEM_LIMIT = 56 * 1024 * 1024


def _nt_dot(a, b):
    return lax.dot_general(a, b, (((1,), (1,)), ((), ())), preferred_element_type=F32)


def _dot(a, b):
    return jnp.dot(a, b, preferred_element_type=F32)


def _rms_rows(x, g):
    ms = jnp.mean(x * x, axis=-1, keepdims=True)
    return (x * lax.rsqrt(ms + EPS)) * g


def _proj_in_kernel(h_ref, g_ref, w_ref, wfg_ref, fb_ref, gain_ref, bd_ref, tri_ref,
                    z_ref, f_ref, hn_scr, carry_scr, *, tiles_per_batch, n_norm_blocks, tn):
    i = pl.program_id(0)
    j = pl.program_id(1)
    tm = h_ref.shape[0]

    @pl.when(j == 0)
    def _():
        hnb = _rms_rows(h_ref[...], g_ref[...]).astype(BF16)
        hn_scr[...] = hnb
        x = _nt_dot(wfg_ref[...], hnb) + fb_ref[:, 0:1]
        logf = jnp.minimum(x, 0.0) - jnp.log1p(jnp.exp(-jnp.abs(x)))
        hi = logf.astype(BF16)
        r1 = logf - hi.astype(F32)
        mid = r1.astype(BF16)
        lo = (r1 - mid.astype(F32)).astype(BF16)
        tri = tri_ref[...]
        cs = _dot(hi, tri) + _dot(mid, tri) + _dot(lo, tri)

        @pl.when(i % tiles_per_batch == 0)
        def _():
            carry_scr[...] = jnp.zeros_like(carry_scr)

        fcum = cs + carry_scr[:, 0:1]
        f_ref[0] = fcum
        carry_scr[...] = jnp.broadcast_to(fcum[:, tm - 1:tm], carry_scr.shape)

    z = _dot(hn_scr[...], w_ref[...])

    @pl.when(j < n_norm_blocks)
    def _():
        bd = bd_ref[...]
        w = bd.shape[0]
        for s in range(tn // w):
            zz = z[:, s * w:(s + 1) * w]
            ss = _dot((zz * zz).astype(BF16), bd)
            inv = lax.rsqrt(ss * (1.0 / HEAD_DIM) + EPS)
            z_ref[:, s * w:(s + 1) * w] = (zz * inv * gain_ref[:, s * w:(s + 1) * w]).astype(BF16)

    @pl.when(j >= n_norm_blocks)
    def _():
        z_ref[...] = z.astype(BF16)


def _proj_in(h, g, w, wfg, fb, gain, bd, tri, *, batch, seq, tm, tn):
    t = h.shape[0]
    tiles_per_batch = seq // tm
    n_norm_blocks = Z_NORM_COLS // tn
    kern = functools.partial(_proj_in_kernel, tiles_per_batch=tiles_per_batch,
                             n_norm_blocks=n_norm_blocks, tn=tn)
    return pl.pallas_call(
        kern,
        grid=(t // tm, Z_COLS // tn),
        in_specs=[
            pl.BlockSpec((tm, D_MODEL), lambda i, j: (i, 0)),
            pl.BlockSpec((1, D_MODEL), lambda i, j: (0, 0)),
            pl.BlockSpec((D_MODEL, tn), lambda i, j: (0, j)),
            pl.BlockSpec((FG_ROWS, D_MODEL), lambda i, j: (0, 0)),
            pl.BlockSpec((FG_ROWS, LANES), lambda i, j: (0, 0)),
            pl.BlockSpec((1, tn), lambda i, j: (0, j)),
            pl.BlockSpec(bd.shape, lambda i, j: (0, 0)),
            pl.BlockSpec((tm, tm), lambda i, j: (0, 0)),
        ],
        out_specs=[
            pl.BlockSpec((tm, tn), lambda i, j: (i, j)),
            pl.BlockSpec((1, FG_ROWS, tm), lambda i, j: (i // tiles_per_batch, 0, i % tiles_per_batch)),
        ],
        out_shape=[
            jax.ShapeDtypeStruct((t, Z_COLS), BF16),
            jax.ShapeDtypeStruct((batch, FG_ROWS, seq), F32),
        ],
        scratch_shapes=[
            pltpu.VMEM((tm, D_MODEL), BF16),
            pltpu.VMEM((FG_ROWS, LANES), F32),
        ],
        compiler_params=pltpu.CompilerParams(
            dimension_semantics=("arbitrary", "arbitrary"), vmem_limit_bytes=VMEM_LIMIT),
        name="proj_in",
    )(h, g, w, wfg, fb, gain, bd, tri)


def _head_lane_mask(h):
    lane = lax.broadcasted_iota(jnp.int32, (1, LANES), 1)
    return (lane >= h * HEAD_DIM) & (lane < (h + 1) * HEAD_DIM)


def _online_update(carry, s, vt):
    m, l, acc = carry
    m_new = jnp.maximum(m, jnp.max(s, axis=-1, keepdims=True))
    alpha = jnp.exp(m - m_new)
    p = jnp.exp(s - m_new)
    l_new = alpha * l + jnp.sum(p, axis=-1, keepdims=True)
    acc_new = alpha * acc + _dot(p.astype(BF16), vt)
    return m_new, l_new, acc_new


def _init_carry(tq):
    return (jnp.full((tq, 1), NEG_INF, F32), jnp.zeros((tq, 1), F32), jnp.zeros((tq, LANES), F32))


def _fox_kernel(q_ref, k_ref, v_ref, f_ref, o_ref, *, tq, tk):
    hp = pl.program_id(1)
    qi = pl.program_id(2)
    q = q_ref[0]
    row = qi * tq + lax.broadcasted_iota(jnp.int32, (tq, 1), 0)
    n_full = (qi * tq) // tk
    n_all = ((qi + 1) * tq + tk - 1) // tk
    out = jnp.zeros((tq, LANES), F32)
    for h in range(HEADS_PER_BLOCK):
        hm = _head_lane_mask(h)
        qh = jnp.where(hm, q, jnp.zeros_like(q))
        head = hp * HEADS_PER_BLOCK + h

        def scores(j):
            k0 = pl.multiple_of(j * tk, tk)
            s = _nt_dot(qh, k_ref[0, pl.ds(k0, tk), :])
            s = s - f_ref[0, pl.ds(head, 1), pl.ds(k0, tk)]
            v = v_ref[0, pl.ds(k0, tk), :]
            return s, jnp.where(hm, v, jnp.zeros_like(v)), k0

        def body_full(j, carry):
            s, vt, _ = scores(j)
            return _online_update(carry, s, vt)

        def body_diag(j, carry):
            s, vt, k0 = scores(j)
            col = k0 + lax.broadcasted_iota(jnp.int32, (1, tk), 1)
            s = jnp.where(col <= row, s, NEG_INF)
            return _online_update(carry, s, vt)

        carry = lax.fori_loop(0, n_full, body_full, _init_carry(tq))
        m, l, acc = lax.fori_loop(n_full, n_all, body_diag, carry)
        out = out + acc / l
    o_ref[0] = out.astype(BF16)


def _fox_attention(z, fcum, *, tq, tk):
    b, s, _ = z.shape
    nblk = W_FOX // LANES
    return pl.pallas_call(
        functools.partial(_fox_kernel, tq=tq, tk=tk),
        grid=(b, nblk, s // tq),
        in_specs=[
            pl.BlockSpec((1, tq, LANES), lambda bi, hp, qi: (bi, qi, ZB_QF + hp)),
            pl.BlockSpec((1, s, LANES), lambda bi, hp, qi: (bi, 0, ZB_KF + hp)),
            pl.BlockSpec((1, s, LANES), lambda bi, hp, qi: (bi, 0, ZB_VF + hp)),
            pl.BlockSpec((1, FG_ROWS, s), lambda bi, hp, qi: (bi, 0, 0)),
        ],
        out_specs=pl.BlockSpec((1, tq, LANES), lambda bi, hp, qi: (bi, qi, hp)),
        out_shape=jax.ShapeDtypeStruct((b, s, W_FOX), BF16),
        compiler_params=pltpu.CompilerParams(
            dimension_semantics=("parallel", "parallel", "arbitrary"), vmem_limit_bytes=VMEM_LIMIT),
        name="fox_attn",
    )(z, z, z, fcum)


def _diff_kernel(q1_ref, q2_ref, k1_ref, k2_ref, v_ref, lp_ref, sg_ref, o_ref, *, tq, tk, lam_init):
    hp = pl.program_id(1)
    qi = pl.program_id(2)
    q1 = q1_ref[0]
    q2 = q2_ref[0]
    row = qi * tq + lax.broadcasted_iota(jnp.int32, (tq, 1), 0)
    n_full = (qi * tq) // tk
    n_all = ((qi + 1) * tq + tk - 1) // tk

    lp = lp_ref[...]
    lam = (jnp.exp(jnp.sum(lp[0:1] * lp[1:2], axis=-1, keepdims=True))
           - jnp.exp(jnp.sum(lp[2:3] * lp[3:4], axis=-1, keepdims=True)) + lam_init)

    out = jnp.zeros((tq, LANES), F32)
    for h in range(HEADS_PER_BLOCK):
        hm = _head_lane_mask(h)
        q1h = jnp.where(hm, q1, jnp.zeros_like(q1))
        q2h = jnp.where(hm, q2, jnp.zeros_like(q2))
        slope = jnp.where(hp == 0, ALIBI_SLOPES[h], ALIBI_SLOPES[HEADS_PER_BLOCK + h]).astype(F32)

        def tiles(j):
            k0 = pl.multiple_of(j * tk, tk)
            s1 = _nt_dot(q1h, k1_ref[0, pl.ds(k0, tk), :])
            s2 = _nt_dot(q2h, k2_ref[0, pl.ds(k0, tk), :])
            v = v_ref[0, pl.ds(k0, tk), :]
            col = k0 + lax.broadcasted_iota(jnp.int32, (1, tk), 1)
            return s1, s2, jnp.where(hm, v, jnp.zeros_like(v)), col

        def body_full(j, carry):
            c1, c2 = carry
            s1, s2, vt, col = tiles(j)
            bias = slope * col.astype(F32)
            return _online_update(c1, s1 + bias, vt), _online_update(c2, s2 + bias, vt)

        def body_diag(j, carry):
            c1, c2 = carry
            s1, s2, vt, col = tiles(j)
            bias = slope * jnp.minimum(col, 2 * row - col).astype(F32)
            allowed = (col // CHUNK) <= (row // CHUNK)
            s1 = jnp.where(allowed, s1 + bias, NEG_INF)
            s2 = jnp.where(allowed, s2 + bias, NEG_INF)
            return _online_update(c1, s1, vt), _online_update(c2, s2, vt)

        carry = lax.fori_loop(0, n_full, body_full, (_init_carry(tq), _init_carry(tq)))
        (_, l1, a1), (_, l2, a2) = lax.fori_loop(n_full, n_all, body_diag, carry)
        out = out + (a1 / l1 - lam * (a2 / l2))

    sq = out * out
    inv = jnp.zeros((tq, LANES), F32)
    for h in range(HEADS_PER_BLOCK):
        hm = _head_lane_mask(h)
        ms = jnp.sum(jnp.where(hm, sq, 0.0), axis=-1, keepdims=True) * (1.0 / HEAD_DIM)
        inv = jnp.where(hm, lax.rsqrt(ms + EPS), inv)
    o_ref[0] = ((out * inv) * sg_ref[...] * (1.0 - lam_init)).astype(BF16)


def _diff_attention(z, lam_params, subln2, *, tq, tk, lam_init):
    b, s, _ = z.shape
    nblk = W_DIFF // LANES
    return pl.pallas_call(
        functools.partial(_diff_kernel, tq=tq, tk=tk, lam_init=lam_init),
        grid=(b, nblk, s // tq),
        in_specs=[
            pl.BlockSpec((1, tq, LANES), lambda bi, hp, qi: (bi, qi, ZB_Q1 + hp)),
            pl.BlockSpec((1, tq, LANES), lambda bi, hp, qi: (bi, qi, ZB_Q2 + hp)),
            pl.BlockSpec((1, s, LANES), lambda bi, hp, qi: (bi, 0, ZB_K1 + hp)),
            pl.BlockSpec((1, s, LANES), lambda bi, hp, qi: (bi, 0, ZB_K2 + hp)),
            pl.BlockSpec((1, s, LANES), lambda bi, hp, qi: (bi, 0, ZB_VA + hp)),
            pl.BlockSpec((4, HEAD_DIM), lambda bi, hp, qi: (0, 0)),
            pl.BlockSpec((1, LANES), lambda bi, hp, qi: (0, 0)),
        ],
        out_specs=pl.BlockSpec((1, tq, LANES), lambda bi, hp, qi: (bi, qi, hp)),
        out_shape=jax.ShapeDtypeStruct((b, s, W_DIFF), BF16),
        compiler_params=pltpu.CompilerParams(
            dimension_semantics=("parallel", "parallel", "arbitrary"), vmem_limit_bytes=VMEM_LIMIT),
        name="diff_attn",
    )(z, z, z, z, z, lam_params, subln2)


CHUNK_TILE = 256
CHUNK_NTILES = 1 + (N_PREV_CHUNKS * CHUNK) // CHUNK_TILE


def _chunk_kernel(q_ref, k_ref, v_ref, bm_ref, o_ref):
    qi = pl.program_id(2)
    t = CHUNK_TILE
    q = q_ref[0]
    out = jnp.zeros((t, LANES), F32)
    for h in range(HEADS_PER_BLOCK):
        hm = _head_lane_mask(h)
        qh = jnp.where(hm, q, jnp.zeros_like(q))
        ss = []
        vs = []
        for d in range(CHUNK_NTILES):
            kt = qi - d
            k0 = pl.multiple_of(jnp.maximum(kt, 0) * t, t)
            s = _nt_dot(qh, k_ref[0, pl.ds(k0, t), :]) + bm_ref[h, d]
            s = jnp.where(kt >= 0, s, NEG_INF)
            ss.append(s)
            v = v_ref[0, pl.ds(k0, t), :]
            vs.append(jnp.where(hm, v, jnp.zeros_like(v)))
        m = ss[0].max(axis=-1, keepdims=True)
        for s in ss[1:]:
            m = jnp.maximum(m, s.max(axis=-1, keepdims=True))
        l = jnp.zeros((t, 1), F32)
        acc = jnp.zeros((t, LANES), F32)
        for s, vt in zip(ss, vs):
            p = jnp.exp(s - m)
            l = l + jnp.sum(p, axis=-1, keepdims=True)
            acc = acc + _dot(p.astype(BF16), vt)
        out = out + acc / l
    o_ref[0] = out.astype(BF16)


def _chunk_attention(z, bm):
    b, s, _ = z.shape
    t = CHUNK_TILE
    nblk = W_CHUNK // LANES
    return pl.pallas_call(
        _chunk_kernel,
        grid=(b, nblk, s // t),
        in_specs=[
            pl.BlockSpec((1, t, LANES), lambda bi, hp, qi: (bi, qi, ZB_QC + hp)),
            pl.BlockSpec((1, s, LANES), lambda bi, hp, qi: (bi, 0, ZB_KC + hp)),
            pl.BlockSpec((1, s, LANES), lambda bi, hp, qi: (bi, 0, ZB_VC + hp)),
            pl.BlockSpec((HEADS_PER_BLOCK, CHUNK_NTILES, t, t), lambda bi, hp, qi: (hp, 0, 0, 0)),
        ],
        out_specs=pl.BlockSpec((1, t, LANES), lambda bi, hp, qi: (bi, qi, hp)),
        out_shape=jax.ShapeDtypeStruct((b, s, W_CHUNK), BF16),
        compiler_params=pltpu.CompilerParams(
            dimension_semantics=("parallel", "parallel", "arbitrary"), vmem_limit_bytes=VMEM_LIMIT),
        name="chunk_attn",
    )(z, z, z, bm)


def _chunk_bias_table(rel_table):
    t = CHUNK_TILE
    a = np.arange(t)[:, None]
    b = np.arange(t)[None, :]
    idx = []
    valid = []
    for d in range(CHUNK_NTILES):
        rel = a - b + t * d
        idx.append(np.clip(rel, -REL_CLIP, REL_CLIP) + REL_CLIP)
        qc = a // CHUNK + (t // CHUNK) * d
        kc = b // CHUNK
        valid.append((kc <= qc) & (kc >= qc - N_PREV_CHUNKS))
    idx = np.stack(idx)
    valid = np.stack(valid)
    return jnp.where(valid[None], rel_table.astype(F32)[:, idx], NEG_INF)


def _out_proj_kernel(h_ref, oa_ref, ob_ref, oc_ref, w_ref, o_ref, cat_scr):
    cat_scr[:, 0:W_DIFF] = oa_ref[...]
    cat_scr[:, W_DIFF:W_DIFF + W_FOX] = ob_ref[...]
    cat_scr[:, W_DIFF + W_FOX:MIX_WIDTH] = oc_ref[...]
    o_ref[...] = h_ref[...] + _dot(cat_scr[...], w_ref[...])


def _out_proj(h, oa, ob, oc, w, *, tm):
    t = h.shape[0]
    return pl.pallas_call(
        _out_proj_kernel,
        grid=(t // tm,),
        in_specs=[
            pl.BlockSpec((tm, D_MODEL), lambda i: (i, 0)),
            pl.BlockSpec((tm, W_DIFF), lambda i: (i, 0)),
            pl.BlockSpec((tm, W_FOX), lambda i: (i, 0)),
            pl.BlockSpec((tm, W_CHUNK), lambda i: (i, 0)),
            pl.BlockSpec((MIX_WIDTH, D_MODEL), lambda i: (0, 0)),
        ],
        out_specs=pl.BlockSpec((tm, D_MODEL), lambda i: (i, 0)),
        out_shape=jax.ShapeDtypeStruct((t, D_MODEL), F32),
        scratch_shapes=[pltpu.VMEM((tm, MIX_WIDTH), BF16)],
        compiler_params=pltpu.CompilerParams(
            dimension_semantics=("parallel",), vmem_limit_bytes=VMEM_LIMIT),
        name="out_proj",
    )(h, oa, ob, oc, w)


MLP_CW = 256
HALO = 8


def _mlp_kernel(h_ref, g_ref, wup_ref, cw_ref, cb_ref, wdn_ref, o_ref,
                hn_scr, ubuf, tail, acc, *, tiles_per_batch):
    i = pl.program_id(0)
    tm = h_ref.shape[0]

    @pl.when(i % tiles_per_batch == 0)
    def _():
        tail[...] = jnp.zeros_like(tail)

    hn_scr[...] = _rms_rows(h_ref[...], g_ref[...]).astype(BF16)

    def conv(c0):
        cols = slice(c0, c0 + MLP_CW)
        u = _dot(hn_scr[...], wup_ref[:, cols])
        ubuf[0:HALO, :] = tail[:, cols]
        ubuf[HALO:HALO + tm, :] = u
        tail[:, cols] = u[tm - HALO:tm, :]
        return (cb_ref[:, cols]
                + cw_ref[0:1, cols] * ubuf[HALO - 2:HALO - 2 + tm, :]
                + cw_ref[1:2, cols] * ubuf[HALO - 1:HALO - 1 + tm, :]
                + cw_ref[2:3, cols] * u)

    for c in range(D_FF // MLP_CW):
        gate = conv(c * MLP_CW)
        val = conv(D_FF + c * MLP_CW)
        act = (gate * jax.nn.sigmoid(gate) * val).astype(BF16)
        part = _dot(act, wdn_ref[c * MLP_CW:(c + 1) * MLP_CW, :])
        if c == 0:
            acc[...] = part
        else:
            acc[...] += part
    o_ref[...] = h_ref[...] + acc[...]


def _mlp(h, g, wup, cw, cb, wdn, *, seq, tm):
    t = h.shape[0]
    const = lambda i: (0, 0)
    return pl.pallas_call(
        functools.partial(_mlp_kernel, tiles_per_batch=seq // tm),
        grid=(t // tm,),
        in_specs=[
            pl.BlockSpec((tm, D_MODEL), lambda i: (i, 0)),
            pl.BlockSpec((1, D_MODEL), const),
            pl.BlockSpec((D_MODEL, 2 * D_FF), const, pipeline_mode=pl.Buffered(1)),
            pl.BlockSpec((CONV_WIDTH, 2 * D_FF), const),
            pl.BlockSpec((1, 2 * D_FF), const),
            pl.BlockSpec((D_FF, D_MODEL), const, pipeline_mode=pl.Buffered(1)),
        ],
        out_specs=pl.BlockSpec((tm, D_MODEL), lambda i: (i, 0)),
        out_shape=jax.ShapeDtypeStruct((t, D_MODEL), F32),
        scratch_shapes=[
            pltpu.VMEM((tm, D_MODEL), BF16),
            pltpu.VMEM((HALO + tm, MLP_CW), F32),
            pltpu.VMEM((HALO, 2 * D_FF), F32),
            pltpu.VMEM((tm, D_MODEL), F32),
        ],
        compiler_params=pltpu.CompilerParams(
            dimension_semantics=("arbitrary",), vmem_limit_bytes=VMEM_LIMIT),
        name="conv_mlp",
    )(h, g, wup, cw, cb, wdn)


def _ple_kernel(h_ref, p_ref, g_ref, wg_ref, wp_ref, o_ref):
    h = h_ref[...]
    hn = _rms_rows(h, g_ref[...]).astype(BF16)
    gate = jax.nn.sigmoid(_dot(hn, wg_ref[...]))
    proj = _dot(p_ref[...].astype(BF16), wp_ref[...])
    o_ref[...] = h + proj * gate


def _ple(h, p, g, wg, wp, *, tm):
    t = h.shape[0]
    const = lambda i: (0, 0)
    return pl.pallas_call(
        _ple_kernel,
        grid=(t // tm,),
        in_specs=[
            pl.BlockSpec((tm, D_MODEL), lambda i: (i, 0)),
            pl.BlockSpec((tm, PLE_DIM), lambda i: (i, 0)),
            pl.BlockSpec((1, D_MODEL), const),
            pl.BlockSpec((D_MODEL, D_MODEL), const),
            pl.BlockSpec((PLE_DIM, D_MODEL), const),
        ],
        out_specs=pl.BlockSpec((tm, D_MODEL), lambda i: (i, 0)),
        out_shape=jax.ShapeDtypeStruct((t, D_MODEL), F32),
        compiler_params=pltpu.CompilerParams(
            dimension_semantics=("parallel",), vmem_limit_bytes=VMEM_LIMIT),
        name="ple",
    )(h, p, g, wg, wp)


def _prep_w_in(w_in, qk_gain):
    sizes = (W_DIFF,) * 5 + (W_FOX,) * 3 + (H_FOX,) + (W_CHUNK,) * 3
    offs = np.cumsum((0,) + sizes)
    seg = lambda n: w_in[:, offs[n]:offs[n + 1]]
    q1, q2, k1, k2, va, qf, kf, vf, fg, qc, kc, vc = (seg(n) for n in range(12))
    w = jnp.concatenate([q1, q2, k1, k2, qf, kf, qc, kc, va, vf, vc], axis=1).astype(BF16)
    wfg = jnp.zeros((FG_ROWS, D_MODEL), F32).at[:H_FOX].set(fg.T).astype(BF16)
    scale = HEAD_DIM ** -0.5
    g = qk_gain.astype(F32)
    gains = jnp.concatenate([
        jnp.tile(g[0] * scale, H_DIFF), jnp.tile(g[0] * scale, H_DIFF),
        jnp.tile(g[1], H_DIFF), jnp.tile(g[1], H_DIFF),
        jnp.tile(g[2] * scale, H_FOX), jnp.tile(g[3], H_FOX),
        jnp.tile(g[4] * scale, H_CHUNK), jnp.tile(g[5], H_CHUNK),
        jnp.ones((Z_COLS - Z_NORM_COLS,), F32)])[None, :]
    return w, wfg, gains


def _block_diag_ones(width):
    r = np.arange(width)
    return jnp.asarray((r[:, None] // HEAD_DIM) == (r[None, :] // HEAD_DIM), BF16)


def _pick(n, pref):
    while n % pref:
        pref //= 2
    return pref


def kernel(x, p, ln_mix, w_in, qk_gain, lam_params, subln_gain, fgate_bias, rel_bias, w_out,
           ln_ffn, w_up, conv_w, conv_b, w_down, ln_ple, w_ple_gate, w_ple_proj):
    b, s, d = x.shape
    depth = w_in.shape[0]
    t = b * s
    tm_proj = _pick(s, 512)
    tn_proj = 512
    tm_mlp = _pick(s, 512)
    tm_row = _pick(s, 512)
    tq = _pick(s, 256)
    tk = _pick(s, 512)

    bd = _block_diag_ones(256)
    tri = jnp.asarray(np.triu(np.ones((tm_proj, tm_proj), np.float32)), BF16)

    h = x.reshape(t, d).astype(F32)
    for i in range(depth):
        lam_init = 0.8 - 0.6 * math.exp(-0.3 * i)
        w, wfg, gains = _prep_w_in(w_in[i], qk_gain[i])
        fb = jnp.zeros((FG_ROWS, LANES), F32).at[:H_FOX, :].set(
            jnp.broadcast_to(fgate_bias[i].astype(F32)[:, None], (H_FOX, LANES)))
        z, fcum = _proj_in(h, ln_mix[i][None, :].astype(F32), w, wfg, fb, gains, bd, tri,
                           batch=b, seq=s, tm=tm_proj, tn=tn_proj)
        z = z.reshape(b, s, Z_COLS)
        subln2 = jnp.tile(subln_gain[i].astype(F32), HEADS_PER_BLOCK)[None, :]
        oa = _diff_attention(z, lam_params[i].astype(F32), subln2, tq=tq, tk=tk, lam_init=lam_init)
        ob = _fox_attention(z, fcum, tq=tq, tk=tk)
        oc = _chunk_attention(z, _chunk_bias_table(rel_bias[i]))
        h = _out_proj(h, oa.reshape(t, W_DIFF), ob.reshape(t, W_FOX), oc.reshape(t, W_CHUNK),
                      w_out[i].astype(BF16), tm=tm_row)
        h = _mlp(h, ln_ffn[i][None, :].astype(F32), w_up[i].astype(BF16), conv_w[i].astype(F32),
                 conv_b[i][None, :].astype(F32), w_down[i].astype(BF16), seq=s, tm=tm_mlp)
        h = _ple(h, p[i].reshape(t, PLE_DIM), ln_ple[i][None, :].astype(F32),
                 w_ple_gate[i].astype(BF16), w_ple_proj[i].astype(BF16), tm=tm_row)
    return h.reshape(b, s, d).astype(x.dtype)
```

```python
import functools
import math

import numpy as np
import jax
import jax.numpy as jnp
from jax import lax
from jax.experimental import pallas as pl
from jax.experimental.pallas import tpu as pltpu

F32 = jnp.float32
BF16 = jnp.bfloat16

D_MODEL = 1024
CHUNK = 64
HEAD_DIM = 64
H_DIFF = 4
H_FOX = 6
H_CHUNK = 6
N_PREV_CHUNKS = 8
REL_CLIP = 128
D_FF = 2816
CONV_WIDTH = 3
PLE_DIM = 256
EPS = 1e-6
NEG_INF = -1e30
LOG2E = math.log2(math.e)

W_DIFF = H_DIFF * HEAD_DIM
W_FOX = H_FOX * HEAD_DIM
W_CHUNK = H_CHUNK * HEAD_DIM
MIX_WIDTH = W_DIFF + W_FOX + W_CHUNK
ALIBI_SLOPES = tuple(2.0 ** (-8.0 * (h + 1) / H_DIFF) for h in range(H_DIFF))

LANES = 128
HEADS_PER_BLOCK = LANES // HEAD_DIM
FG_ROWS = 8

_Z_SIZES = (W_DIFF,) * 4 + (W_FOX,) * 2 + (W_CHUNK,) * 2 + (W_DIFF, W_FOX, W_CHUNK)
_Z_OFFS = tuple(int(c) for c in np.cumsum((0,) + _Z_SIZES))
Z_COLS = _Z_OFFS[-1]
Z_NORM_COLS = _Z_OFFS[8]
(ZB_Q1, ZB_Q2, ZB_K1, ZB_K2, ZB_QF, ZB_KF, ZB_QC, ZB_KC, ZB_VA, ZB_VF, ZB_VC) = (
    o // LANES for o in _Z_OFFS[:-1])

VMEM_LIMIT = 56 * 1024 * 1024


def _nt_dot(a, b):
    return lax.dot_general(a, b, (((1,), (1,)), ((), ())), preferred_element_type=F32)


def _dot(a, b):
    return jnp.dot(a, b, preferred_element_type=F32)


def _rms_rows(x, g):
    ms = jnp.mean(x * x, axis=-1, keepdims=True)
    return (x * lax.rsqrt(ms + EPS)) * g


def _proj_in_kernel(h_ref, g_ref, w_ref, wfg_ref, fb_ref, gain_ref, bd_ref, tri_ref,
                    z_ref, f_ref, hn_scr, carry_scr, *, tiles_per_batch, n_norm_blocks, tn):
    i = pl.program_id(0)
    j = pl.program_id(1)
    tm = h_ref.shape[0]

    @pl.when(j == 0)
    def _():
        hnb = _rms_rows(h_ref[...], g_ref[...]).astype(BF16)
        hn_scr[...] = hnb
        x = _nt_dot(wfg_ref[...], hnb) + fb_ref[:, 0:1]
        logf = jnp.minimum(x, 0.0) - jnp.log1p(jnp.exp(-jnp.abs(x)))
        hi = logf.astype(BF16)
        r1 = logf - hi.astype(F32)
        mid = r1.astype(BF16)
        lo = (r1 - mid.astype(F32)).astype(BF16)
        tri = tri_ref[...]
        cs = _dot(hi, tri) + _dot(mid, tri) + _dot(lo, tri)

        @pl.when(i % tiles_per_batch == 0)
        def _():
            carry_scr[...] = jnp.zeros_like(carry_scr)

        fcum = cs + carry_scr[:, 0:1]
        f_ref[0] = fcum * LOG2E
        carry_scr[...] = jnp.broadcast_to(fcum[:, tm - 1:tm], carry_scr.shape)

    z = _dot(hn_scr[...], w_ref[...])

    @pl.when(j < n_norm_blocks)
    def _():
        bd = bd_ref[...]
        w = bd.shape[0]
        for s in range(tn // w):
            zz = z[:, s * w:(s + 1) * w]
            ss = _dot((zz * zz).astype(BF16), bd)
            inv = lax.rsqrt(ss * (1.0 / HEAD_DIM) + EPS)
            z_ref[:, s * w:(s + 1) * w] = (zz * inv * gain_ref[:, s * w:(s + 1) * w]).astype(BF16)

    @pl.when(j >= n_norm_blocks)
    def _():
        z_ref[...] = z.astype(BF16)


def _proj_in(h, g, w, wfg, fb, gain, bd, tri, *, batch, seq, tm, tn):
    t = h.shape[0]
    tiles_per_batch = seq // tm
    n_norm_blocks = Z_NORM_COLS // tn
    kern = functools.partial(_proj_in_kernel, tiles_per_batch=tiles_per_batch,
                             n_norm_blocks=n_norm_blocks, tn=tn)
    return pl.pallas_call(
        kern,
        grid=(t // tm, Z_COLS // tn),
        in_specs=[
            pl.BlockSpec((tm, D_MODEL), lambda i, j: (i, 0)),
            pl.BlockSpec((1, D_MODEL), lambda i, j: (0, 0)),
            pl.BlockSpec((D_MODEL, tn), lambda i, j: (0, j)),
            pl.BlockSpec((FG_ROWS, D_MODEL), lambda i, j: (0, 0)),
            pl.BlockSpec((FG_ROWS, LANES), lambda i, j: (0, 0)),
            pl.BlockSpec((1, tn), lambda i, j: (0, j)),
            pl.BlockSpec(bd.shape, lambda i, j: (0, 0)),
            pl.BlockSpec((tm, tm), lambda i, j: (0, 0)),
        ],
        out_specs=[
            pl.BlockSpec((tm, tn), lambda i, j: (i, j)),
            pl.BlockSpec((1, FG_ROWS, tm), lambda i, j: (i // tiles_per_batch, 0, i % tiles_per_batch)),
        ],
        out_shape=[
            jax.ShapeDtypeStruct((t, Z_COLS), BF16),
            jax.ShapeDtypeStruct((batch, FG_ROWS, seq), F32),
        ],
        scratch_shapes=[
            pltpu.VMEM((tm, D_MODEL), BF16),
            pltpu.VMEM((FG_ROWS, LANES), F32),
        ],
        compiler_params=pltpu.CompilerParams(
            dimension_semantics=("arbitrary", "arbitrary"), vmem_limit_bytes=VMEM_LIMIT),
        name="proj_in",
    )(h, g, w, wfg, fb, gain, bd, tri)


def _head_lane_mask(h):
    lane = lax.broadcasted_iota(jnp.int32, (1, LANES), 1)
    return (lane >= h * HEAD_DIM) & (lane < (h + 1) * HEAD_DIM)


def _ones_lane_mask(h):
    lane = lax.broadcasted_iota(jnp.int32, (1, LANES), 1)
    return lane == ((h + 1) % HEADS_PER_BLOCK) * HEAD_DIM


def _values_with_ones(v, h):
    fill = jnp.where(_ones_lane_mask(h), 1.0, 0.0).astype(v.dtype)
    return jnp.where(_head_lane_mask(h), v, jnp.broadcast_to(fill, v.shape))


def _online_update(carry, s, vt):
    m, acc = carry
    m_new = jnp.maximum(m, jnp.max(s, axis=-1, keepdims=True))
    alpha = jnp.exp2(m - m_new)
    p = jnp.exp2(s - m_new).astype(BF16)
    return m_new, alpha * acc + _dot(p, vt)


def _init_carry(tq):
    return (jnp.full((tq, 1), NEG_INF, F32), jnp.zeros((tq, LANES), F32))


def _normalised(acc, h):
    l = jnp.sum(jnp.where(_ones_lane_mask(h), acc, 0.0), axis=-1, keepdims=True)
    return jnp.where(_head_lane_mask(h), acc / l, 0.0)


def _fox_kernel(q_ref, k_ref, v_ref, f_ref, o_ref, *, tq, tk):
    hp = pl.program_id(1)
    qi = pl.program_id(2)
    q = q_ref[0]
    row = qi * tq + lax.broadcasted_iota(jnp.int32, (tq, 1), 0)
    n_full = (qi * tq) // tk
    n_all = ((qi + 1) * tq + tk - 1) // tk
    heads = range(HEADS_PER_BLOCK)
    qh = [jnp.where(_head_lane_mask(h), q, jnp.zeros_like(q)) for h in heads]

    def tile(j, carry, diag):
        k0 = pl.multiple_of(j * tk, tk)
        kt = k_ref[0, pl.ds(k0, tk), :]
        v = v_ref[0, pl.ds(k0, tk), :]
        out = []
        for h in heads:
            s = _nt_dot(qh[h], kt) - f_ref[0, pl.ds(hp * HEADS_PER_BLOCK + h, 1), pl.ds(k0, tk)]
            if diag:
                col = k0 + lax.broadcasted_iota(jnp.int32, (1, tk), 1)
                s = jnp.where(col <= row, s, NEG_INF)
            out.append(_online_update(carry[h], s, _values_with_ones(v, h)))
        return tuple(out)

    carry = tuple(_init_carry(tq) for _ in heads)
    carry = lax.fori_loop(0, n_full, functools.partial(tile, diag=False), carry)
    carry = lax.fori_loop(n_full, n_all, functools.partial(tile, diag=True), carry)
    out = _normalised(carry[0][1], 0)
    for h in heads[1:]:
        out = out + _normalised(carry[h][1], h)
    o_ref[0] = out.astype(BF16)


def _fox_attention(z, fcum, *, tq, tk):
    b, s, _ = z.shape
    nblk = W_FOX // LANES
    return pl.pallas_call(
        functools.partial(_fox_kernel, tq=tq, tk=tk),
        grid=(b, nblk, s // tq),
        in_specs=[
            pl.BlockSpec((1, tq, LANES), lambda bi, hp, qi: (bi, qi, ZB_QF + hp)),
            pl.BlockSpec((1, s, LANES), lambda bi, hp, qi: (bi, 0, ZB_KF + hp)),
            pl.BlockSpec((1, s, LANES), lambda bi, hp, qi: (bi, 0, ZB_VF + hp)),
            pl.BlockSpec((1, FG_ROWS, s), lambda bi, hp, qi: (bi, 0, 0)),
        ],
        out_specs=pl.BlockSpec((1, tq, LANES), lambda bi, hp, qi: (bi, qi, hp)),
        out_shape=jax.ShapeDtypeStruct((b, s, W_FOX), BF16),
        compiler_params=pltpu.CompilerParams(
            dimension_semantics=("parallel", "parallel", "arbitrary"), vmem_limit_bytes=VMEM_LIMIT),
        name="fox_attn",
    )(z, z, z, fcum)


def _diff_kernel(q1_ref, q2_ref, k1_ref, k2_ref, v_ref, lp_ref, sg_ref, o_ref, *, tq, tk, lam_init):
    hp = pl.program_id(1)
    qi = pl.program_id(2)
    q1 = q1_ref[0]
    q2 = q2_ref[0]
    row = qi * tq + lax.broadcasted_iota(jnp.int32, (tq, 1), 0)
    n_full = (qi * tq) // tk
    n_all = ((qi + 1) * tq + tk - 1) // tk
    heads = range(HEADS_PER_BLOCK)

    lp = lp_ref[...]
    lam = (jnp.exp(jnp.sum(lp[0:1] * lp[1:2], axis=-1, keepdims=True))
           - jnp.exp(jnp.sum(lp[2:3] * lp[3:4], axis=-1, keepdims=True)) + lam_init)

    q1h = [jnp.where(_head_lane_mask(h), q1, jnp.zeros_like(q1)) for h in heads]
    q2h = [jnp.where(_head_lane_mask(h), q2, jnp.zeros_like(q2)) for h in heads]
    slope = [jnp.where(hp == 0, ALIBI_SLOPES[h] * LOG2E, ALIBI_SLOPES[HEADS_PER_BLOCK + h] * LOG2E)
             .astype(F32) for h in heads]

    def tile(j, carry, diag):
        k0 = pl.multiple_of(j * tk, tk)
        k1t = k1_ref[0, pl.ds(k0, tk), :]
        k2t = k2_ref[0, pl.ds(k0, tk), :]
        v = v_ref[0, pl.ds(k0, tk), :]
        col = k0 + lax.broadcasted_iota(jnp.int32, (1, tk), 1)
        out = []
        for h in heads:
            vt = _values_with_ones(v, h)
            s1 = _nt_dot(q1h[h], k1t)
            s2 = _nt_dot(q2h[h], k2t)
            if diag:
                bias = slope[h] * jnp.minimum(col, 2 * row - col).astype(F32)
                allowed = (col // CHUNK) <= (row // CHUNK)
                s1 = jnp.where(allowed, s1 + bias, NEG_INF)
                s2 = jnp.where(allowed, s2 + bias, NEG_INF)
            else:
                bias = slope[h] * col.astype(F32)
                s1 = s1 + bias
                s2 = s2 + bias
            c1, c2 = carry[h]
            out.append((_online_update(c1, s1, vt), _online_update(c2, s2, vt)))
        return tuple(out)

    carry = tuple((_init_carry(tq), _init_carry(tq)) for _ in heads)
    carry = lax.fori_loop(0, n_full, functools.partial(tile, diag=False), carry)
    carry = lax.fori_loop(n_full, n_all, functools.partial(tile, diag=True), carry)
    out = jnp.zeros((tq, LANES), F32)
    for h in heads:
        (_, a1), (_, a2) = carry[h]
        out = out + (_normalised(a1, h) - lam * _normalised(a2, h))

    sq = out * out
    inv = jnp.zeros((tq, LANES), F32)
    for h in heads:
        hm = _head_lane_mask(h)
        ms = jnp.sum(jnp.where(hm, sq, 0.0), axis=-1, keepdims=True) * (1.0 / HEAD_DIM)
        inv = jnp.where(hm, lax.rsqrt(ms + EPS), inv)
    o_ref[0] = ((out * inv) * sg_ref[...] * (1.0 - lam_init)).astype(BF16)


def _diff_attention(z, lam_params, subln2, *, tq, tk, lam_init):
    b, s, _ = z.shape
    nblk = W_DIFF // LANES
    return pl.pallas_call(
        functools.partial(_diff_kernel, tq=tq, tk=tk, lam_init=lam_init),
        grid=(b, nblk, s // tq),
        in_specs=[
            pl.BlockSpec((1, tq, LANES), lambda bi, hp, qi: (bi, qi, ZB_Q1 + hp)),
            pl.BlockSpec((1, tq, LANES), lambda bi, hp, qi: (bi, qi, ZB_Q2 + hp)),
            pl.BlockSpec((1, s, LANES), lambda bi, hp, qi: (bi, 0, ZB_K1 + hp)),
            pl.BlockSpec((1, s, LANES), lambda bi, hp, qi: (bi, 0, ZB_K2 + hp)),
            pl.BlockSpec((1, s, LANES), lambda bi, hp, qi: (bi, 0, ZB_VA + hp)),
            pl.BlockSpec((4, HEAD_DIM), lambda bi, hp, qi: (0, 0)),
            pl.BlockSpec((1, LANES), lambda bi, hp, qi: (0, 0)),
        ],
        out_specs=pl.BlockSpec((1, tq, LANES), lambda bi, hp, qi: (bi, qi, hp)),
        out_shape=jax.ShapeDtypeStruct((b, s, W_DIFF), BF16),
        compiler_params=pltpu.CompilerParams(
            dimension_semantics=("parallel", "parallel", "arbitrary"), vmem_limit_bytes=VMEM_LIMIT),
        name="diff_attn",
    )(z, z, z, z, z, lam_params, subln2)


CHUNK_TILE = 256
CHUNK_NTILES = 1 + (N_PREV_CHUNKS * CHUNK) // CHUNK_TILE


def _chunk_kernel(q_ref, k_ref, v_ref, bm_ref, o_ref):
    qi = pl.program_id(2)
    t = CHUNK_TILE
    q = q_ref[0]
    out = jnp.zeros((t, LANES), F32)
    for h in range(HEADS_PER_BLOCK):
        qh = jnp.where(_head_lane_mask(h), q, jnp.zeros_like(q))
        ss = []
        vs = []
        for d in range(CHUNK_NTILES):
            kt = qi - d
            k0 = pl.multiple_of(jnp.maximum(kt, 0) * t, t)
            s = _nt_dot(qh, k_ref[0, pl.ds(k0, t), :]) + bm_ref[h, d]
            ss.append(jnp.where(kt >= 0, s, NEG_INF))
            vs.append(_values_with_ones(v_ref[0, pl.ds(k0, t), :], h))
        m = ss[0].max(axis=-1, keepdims=True)
        for s in ss[1:]:
            m = jnp.maximum(m, s.max(axis=-1, keepdims=True))
        acc = jnp.zeros((t, LANES), F32)
        for s, vt in zip(ss, vs):
            acc = acc + _dot(jnp.exp2(s - m).astype(BF16), vt)
        out = out + _normalised(acc, h)
    o_ref[0] = out.astype(BF16)


def _chunk_attention(z, bm):
    b, s, _ = z.shape
    t = CHUNK_TILE
    nblk = W_CHUNK // LANES
    return pl.pallas_call(
        _chunk_kernel,
        grid=(b, nblk, s // t),
        in_specs=[
            pl.BlockSpec((1, t, LANES), lambda bi, hp, qi: (bi, qi, ZB_QC + hp)),
            pl.BlockSpec((1, s, LANES), lambda bi, hp, qi: (bi, 0, ZB_KC + hp)),
            pl.BlockSpec((1, s, LANES), lambda bi, hp, qi: (bi, 0, ZB_VC + hp)),
            pl.BlockSpec((HEADS_PER_BLOCK, CHUNK_NTILES, t, t), lambda bi, hp, qi: (hp, 0, 0, 0)),
        ],
        out_specs=pl.BlockSpec((1, t, LANES), lambda bi, hp, qi: (bi, qi, hp)),
        out_shape=jax.ShapeDtypeStruct((b, s, W_CHUNK), BF16),
        compiler_params=pltpu.CompilerParams(
            dimension_semantics=("parallel", "parallel", "arbitrary"), vmem_limit_bytes=VMEM_LIMIT),
        name="chunk_attn",
    )(z, z, z, bm)


def _chunk_bias_table(rel_table):
    t = CHUNK_TILE
    period = 2 * t
    u = np.arange(period)
    rel = np.where(u < t, -u, period - u)
    idx = np.stack([np.clip(rel + t * d, -REL_CLIP, REL_CLIP) + REL_CLIP for d in range(CHUNK_NTILES)])
    vec = rel_table.astype(F32)[:, idx] * LOG2E
    flat = jnp.tile(vec, (1, 1, t))[..., :t * (period - 1)]
    toep = flat.reshape(vec.shape[0], CHUNK_NTILES, t, period - 1)[..., :t]
    a = np.arange(t)[:, None]
    b = np.arange(t)[None, :]
    valid = []
    for d in range(CHUNK_NTILES):
        qc = a // CHUNK + (t // CHUNK) * d
        kc = b // CHUNK
        valid.append((kc <= qc) & (kc >= qc - N_PREV_CHUNKS))
    return jnp.where(np.stack(valid)[None], toep, NEG_INF)


def _out_proj_kernel(h_ref, oa_ref, ob_ref, oc_ref, w_ref, o_ref, cat_scr):
    cat_scr[:, 0:W_DIFF] = oa_ref[...]
    cat_scr[:, W_DIFF:W_DIFF + W_FOX] = ob_ref[...]
    cat_scr[:, W_DIFF + W_FOX:MIX_WIDTH] = oc_ref[...]
    o_ref[...] = h_ref[...] + _dot(cat_scr[...], w_ref[...])


def _out_proj(h, oa, ob, oc, w, *, tm):
    t = h.shape[0]
    return pl.pallas_call(
        _out_proj_kernel,
        grid=(t // tm,),
        in_specs=[
            pl.BlockSpec((tm, D_MODEL), lambda i: (i, 0)),
            pl.BlockSpec((tm, W_DIFF), lambda i: (i, 0)),
            pl.BlockSpec((tm, W_FOX), lambda i: (i, 0)),
            pl.BlockSpec((tm, W_CHUNK), lambda i: (i, 0)),
            pl.BlockSpec((MIX_WIDTH, D_MODEL), lambda i: (0, 0)),
        ],
        out_specs=pl.BlockSpec((tm, D_MODEL), lambda i: (i, 0)),
        out_shape=jax.ShapeDtypeStruct((t, D_MODEL), F32),
        scratch_shapes=[pltpu.VMEM((tm, MIX_WIDTH), BF16)],
        compiler_params=pltpu.CompilerParams(
            dimension_semantics=("parallel",), vmem_limit_bytes=VMEM_LIMIT),
        name="out_proj",
    )(h, oa, ob, oc, w)


MLP_CW = 256
HALO = 8


def _mlp_kernel(h_ref, g_ref, wup_ref, cw_ref, cb_ref, wdn_ref, o_ref,
                hn_scr, ubuf, tail, acc, *, tiles_per_batch):
    i = pl.program_id(0)
    tm = h_ref.shape[0]

    @pl.when(i % tiles_per_batch == 0)
    def _():
        tail[...] = jnp.zeros_like(tail)

    hn_scr[...] = _rms_rows(h_ref[...], g_ref[...]).astype(BF16)

    def conv(c0):
        cols = slice(c0, c0 + MLP_CW)
        u = _dot(hn_scr[...], wup_ref[:, cols])
        ubuf[0:HALO, :] = tail[:, cols]
        ubuf[HALO:HALO + tm, :] = u
        tail[:, cols] = u[tm - HALO:tm, :]
        return (cb_ref[:, cols]
                + cw_ref[0:1, cols] * ubuf[HALO - 2:HALO - 2 + tm, :]
                + cw_ref[1:2, cols] * ubuf[HALO - 1:HALO - 1 + tm, :]
                + cw_ref[2:3, cols] * u)

    for c in range(D_FF // MLP_CW):
        gate = conv(c * MLP_CW)
        val = conv(D_FF + c * MLP_CW)
        act = (gate * jax.nn.sigmoid(gate) * val).astype(BF16)
        part = _dot(act, wdn_ref[c * MLP_CW:(c + 1) * MLP_CW, :])
        if c == 0:
            acc[...] = part
        else:
            acc[...] += part
    o_ref[...] = h_ref[...] + acc[...]


def _mlp(h, g, wup, cw, cb, wdn, *, seq, tm):
    t = h.shape[0]
    const = lambda i: (0, 0)
    return pl.pallas_call(
        functools.partial(_mlp_kernel, tiles_per_batch=seq // tm),
        grid=(t // tm,),
        in_specs=[
            pl.BlockSpec((tm, D_MODEL), lambda i: (i, 0)),
            pl.BlockSpec((1, D_MODEL), const),
            pl.BlockSpec((D_MODEL, 2 * D_FF), const, pipeline_mode=pl.Buffered(1)),
            pl.BlockSpec((CONV_WIDTH, 2 * D_FF), const),
            pl.BlockSpec((1, 2 * D_FF), const),
            pl.BlockSpec((D_FF, D_MODEL), const, pipeline_mode=pl.Buffered(1)),
        ],
        out_specs=pl.BlockSpec((tm, D_MODEL), lambda i: (i, 0)),
        out_shape=jax.ShapeDtypeStruct((t, D_MODEL), F32),
        scratch_shapes=[
            pltpu.VMEM((tm, D_MODEL), BF16),
            pltpu.VMEM((HALO + tm, MLP_CW), F32),
            pltpu.VMEM((HALO, 2 * D_FF), F32),
            pltpu.VMEM((tm, D_MODEL), F32),
        ],
        compiler_params=pltpu.CompilerParams(
            dimension_semantics=("arbitrary",), vmem_limit_bytes=VMEM_LIMIT),
        name="conv_mlp",
    )(h, g, wup, cw, cb, wdn)


def _ple_kernel(h_ref, p_ref, g_ref, wg_ref, wp_ref, o_ref):
    h = h_ref[...]
    hn = _rms_rows(h, g_ref[...]).astype(BF16)
    gate = jax.nn.sigmoid(_dot(hn, wg_ref[...]))
    proj = _dot(p_ref[...].astype(BF16), wp_ref[...])
    o_ref[...] = h + proj * gate


def _ple(h, p, g, wg, wp, *, tm):
    t = h.shape[0]
    const = lambda i: (0, 0)
    return pl.pallas_call(
        _ple_kernel,
        grid=(t // tm,),
        in_specs=[
            pl.BlockSpec((tm, D_MODEL), lambda i: (i, 0)),
            pl.BlockSpec((tm, PLE_DIM), lambda i: (i, 0)),
            pl.BlockSpec((1, D_MODEL), const),
            pl.BlockSpec((D_MODEL, D_MODEL), const),
            pl.BlockSpec((PLE_DIM, D_MODEL), const),
        ],
        out_specs=pl.BlockSpec((tm, D_MODEL), lambda i: (i, 0)),
        out_shape=jax.ShapeDtypeStruct((t, D_MODEL), F32),
        compiler_params=pltpu.CompilerParams(
            dimension_semantics=("parallel",), vmem_limit_bytes=VMEM_LIMIT),
        name="ple",
    )(h, p, g, wg, wp)


def _prep_w_in(w_in, qk_gain):
    sizes = (W_DIFF,) * 5 + (W_FOX,) * 3 + (H_FOX,) + (W_CHUNK,) * 3
    offs = np.cumsum((0,) + sizes)
    seg = lambda n: w_in[:, offs[n]:offs[n + 1]]
    q1, q2, k1, k2, va, qf, kf, vf, fg, qc, kc, vc = (seg(n) for n in range(12))
    w = jnp.concatenate([q1, q2, k1, k2, qf, kf, qc, kc, va, vf, vc], axis=1).astype(BF16)
    wfg = jnp.zeros((FG_ROWS, D_MODEL), F32).at[:H_FOX].set(fg.T).astype(BF16)
    scale = HEAD_DIM ** -0.5 * LOG2E
    g = qk_gain.astype(F32)
    gains = jnp.concatenate([
        jnp.tile(g[0] * scale, H_DIFF), jnp.tile(g[0] * scale, H_DIFF),
        jnp.tile(g[1], H_DIFF), jnp.tile(g[1], H_DIFF),
        jnp.tile(g[2] * scale, H_FOX), jnp.tile(g[3], H_FOX),
        jnp.tile(g[4] * scale, H_CHUNK), jnp.tile(g[5], H_CHUNK),
        jnp.ones((Z_COLS - Z_NORM_COLS,), F32)])[None, :]
    return w, wfg, gains


def _block_diag_ones(width):
    r = np.arange(width)
    return jnp.asarray((r[:, None] // HEAD_DIM) == (r[None, :] // HEAD_DIM), BF16)


def _pick(n, pref):
    while n % pref:
        pref //= 2
    return pref


def kernel(x, p, ln_mix, w_in, qk_gain, lam_params, subln_gain, fgate_bias, rel_bias, w_out,
           ln_ffn, w_up, conv_w, conv_b, w_down, ln_ple, w_ple_gate, w_ple_proj):
    b, s, d = x.shape
    depth = w_in.shape[0]
    t = b * s
    tm_proj = _pick(s, 512)
    tn_proj = 512
    tm_mlp = _pick(s, 512)
    tm_row = _pick(s, 512)
    tq = _pick(s, 256)
    tk = _pick(s, 512)

    bd = _block_diag_ones(256)
    tri = jnp.asarray(np.triu(np.ones((tm_proj, tm_proj), np.float32)), BF16)

    h = x.reshape(t, d).astype(F32)
    for i in range(depth):
        lam_init = 0.8 - 0.6 * math.exp(-0.3 * i)
        w, wfg, gains = _prep_w_in(w_in[i], qk_gain[i])
        fb = jnp.zeros((FG_ROWS, LANES), F32).at[:H_FOX, :].set(
            jnp.broadcast_to(fgate_bias[i].astype(F32)[:, None], (H_FOX, LANES)))
        z, fcum = _proj_in(h, ln_mix[i][None, :].astype(F32), w, wfg, fb, gains, bd, tri,
                           batch=b, seq=s, tm=tm_proj, tn=tn_proj)
        z = z.reshape(b, s, Z_COLS)
        subln2 = jnp.tile(subln_gain[i].astype(F32), HEADS_PER_BLOCK)[None, :]
        oa = _diff_attention(z, lam_params[i].astype(F32), subln2, tq=tq, tk=tk, lam_init=lam_init)
        ob = _fox_attention(z, fcum, tq=tq, tk=tk)
        oc = _chunk_attention(z, _chunk_bias_table(rel_bias[i]))
        h = _out_proj(h, oa.reshape(t, W_DIFF), ob.reshape(t, W_FOX), oc.reshape(t, W_CHUNK),
                      w_out[i].astype(BF16), tm=tm_row)
        h = _mlp(h, ln_ffn[i][None, :].astype(F32), w_up[i].astype(BF16), conv_w[i].astype(F32),
                 conv_b[i][None, :].astype(F32), w_down[i].astype(BF16), seq=s, tm=tm_mlp)
        h = _ple(h, p[i].reshape(t, PLE_DIM), ln_ple[i][None, :].astype(F32),
                 w_ple_gate[i].astype(BF16), w_ple_proj[i].astype(BF16), tm=tm_row)
    return h.reshape(b, s, d).astype(x.dtype)
```

```python
import functools
import math

import numpy as np
import jax
import jax.numpy as jnp
from jax import lax
from jax.experimental import pallas as pl
from jax.experimental.pallas import tpu as pltpu

F32 = jnp.float32
BF16 = jnp.bfloat16

D_MODEL = 1024
CHUNK = 64
HEAD_DIM = 64
H_DIFF = 4
H_FOX = 6
H_CHUNK = 6
N_PREV_CHUNKS = 8
REL_CLIP = 128
D_FF = 2816
CONV_WIDTH = 3
PLE_DIM = 256
EPS = 1e-6
NEG_INF = -1e30
LOG2E = math.log2(math.e)

W_DIFF = H_DIFF * HEAD_DIM
W_FOX = H_FOX * HEAD_DIM
W_CHUNK = H_CHUNK * HEAD_DIM
MIX_WIDTH = W_DIFF + W_FOX + W_CHUNK
ALIBI_SLOPES = tuple(2.0 ** (-8.0 * (h + 1) / H_DIFF) for h in range(H_DIFF))

LANES = 128
HEADS_PER_BLOCK = LANES // HEAD_DIM
FG_ROWS = 8

_Z_SIZES = (W_DIFF,) * 4 + (W_FOX,) * 2 + (W_CHUNK,) * 2 + (W_DIFF, W_FOX, W_CHUNK)
_Z_OFFS = tuple(int(c) for c in np.cumsum((0,) + _Z_SIZES))
Z_COLS = _Z_OFFS[-1]
Z_NORM_COLS = _Z_OFFS[8]
(ZB_Q1, ZB_Q2, ZB_K1, ZB_K2, ZB_QF, ZB_KF, ZB_QC, ZB_KC, ZB_VA, ZB_VF, ZB_VC) = (
    o // LANES for o in _Z_OFFS[:-1])

VMEM_LIMIT = 56 * 1024 * 1024


def _nt_dot(a, b):
    return lax.dot_general(a, b, (((1,), (1,)), ((), ())), preferred_element_type=F32)


def _dot(a, b):
    return jnp.dot(a, b, preferred_element_type=F32)


def _rms_rows(x, g):
    ms = jnp.mean(x * x, axis=-1, keepdims=True)
    return (x * lax.rsqrt(ms + EPS)) * g


def _proj_in_kernel(h_ref, g_ref, w_ref, wfg_ref, fb_ref, gain_ref, bd_ref, tri_ref,
                    z_ref, f_ref, hn_scr, carry_scr, *, tiles_per_batch, n_norm_blocks, tn):
    i = pl.program_id(0)
    j = pl.program_id(1)
    tm = h_ref.shape[0]

    @pl.when(j == 0)
    def _():
        hnb = _rms_rows(h_ref[...], g_ref[...]).astype(BF16)
        hn_scr[...] = hnb
        x = _nt_dot(wfg_ref[...], hnb) + fb_ref[:, 0:1]
        logf = jnp.minimum(x, 0.0) - jnp.log1p(jnp.exp(-jnp.abs(x)))
        hi = logf.astype(BF16)
        r1 = logf - hi.astype(F32)
        mid = r1.astype(BF16)
        lo = (r1 - mid.astype(F32)).astype(BF16)
        tri = tri_ref[...]
        cs = _dot(hi, tri) + _dot(mid, tri) + _dot(lo, tri)

        @pl.when(i % tiles_per_batch == 0)
        def _():
            carry_scr[...] = jnp.zeros_like(carry_scr)

        fcum = cs + carry_scr[:, 0:1]
        f_ref[0] = fcum * LOG2E
        carry_scr[...] = jnp.broadcast_to(fcum[:, tm - 1:tm], carry_scr.shape)

    z = _dot(hn_scr[...], w_ref[...])

    @pl.when(j < n_norm_blocks)
    def _():
        bd = bd_ref[...]
        w = bd.shape[0]
        for s in range(tn // w):
            zz = z[:, s * w:(s + 1) * w]
            ss = _dot((zz * zz).astype(BF16), bd)
            inv = lax.rsqrt(ss * (1.0 / HEAD_DIM) + EPS)
            z_ref[:, s * w:(s + 1) * w] = (zz * inv * gain_ref[:, s * w:(s + 1) * w]).astype(BF16)

    @pl.when(j >= n_norm_blocks)
    def _():
        z_ref[...] = z.astype(BF16)


def _proj_in(h, g, w, wfg, fb, gain, bd, tri, *, batch, seq, tm, tn):
    t = h.shape[0]
    tiles_per_batch = seq // tm
    n_norm_blocks = Z_NORM_COLS // tn
    kern = functools.partial(_proj_in_kernel, tiles_per_batch=tiles_per_batch,
                             n_norm_blocks=n_norm_blocks, tn=tn)
    return pl.pallas_call(
        kern,
        grid=(t // tm, Z_COLS // tn),
        in_specs=[
            pl.BlockSpec((tm, D_MODEL), lambda i, j: (i, 0)),
            pl.BlockSpec((1, D_MODEL), lambda i, j: (0, 0)),
            pl.BlockSpec((D_MODEL, tn), lambda i, j: (0, j)),
            pl.BlockSpec((FG_ROWS, D_MODEL), lambda i, j: (0, 0)),
            pl.BlockSpec((FG_ROWS, LANES), lambda i, j: (0, 0)),
            pl.BlockSpec((1, tn), lambda i, j: (0, j)),
            pl.BlockSpec(bd.shape, lambda i, j: (0, 0)),
            pl.BlockSpec((tm, tm), lambda i, j: (0, 0)),
        ],
        out_specs=[
            pl.BlockSpec((tm, tn), lambda i, j: (i, j)),
            pl.BlockSpec((1, FG_ROWS, tm), lambda i, j: (i // tiles_per_batch, 0, i % tiles_per_batch)),
        ],
        out_shape=[
            jax.ShapeDtypeStruct((t, Z_COLS), BF16),
            jax.ShapeDtypeStruct((batch, FG_ROWS, seq), F32),
        ],
        scratch_shapes=[
            pltpu.VMEM((tm, D_MODEL), BF16),
            pltpu.VMEM((FG_ROWS, LANES), F32),
        ],
        compiler_params=pltpu.CompilerParams(
            dimension_semantics=("arbitrary", "arbitrary"), vmem_limit_bytes=VMEM_LIMIT),
        name="proj_in",
    )(h, g, w, wfg, fb, gain, bd, tri)


def _head_lane_mask(h):
    lane = lax.broadcasted_iota(jnp.int32, (1, LANES), 1)
    return (lane >= h * HEAD_DIM) & (lane < (h + 1) * HEAD_DIM)


def _ones_lane_mask(h):
    lane = lax.broadcasted_iota(jnp.int32, (1, LANES), 1)
    return lane == ((h + 1) % HEADS_PER_BLOCK) * HEAD_DIM


def _values_with_ones(v, h):
    fill = jnp.where(_ones_lane_mask(h), 1.0, 0.0).astype(v.dtype)
    return jnp.where(_head_lane_mask(h), v, jnp.broadcast_to(fill, v.shape))


def _online_update(carry, s, vt):
    m, acc = carry
    m_new = jnp.maximum(m, jnp.max(s, axis=-1, keepdims=True))
    alpha = jnp.exp2(m - m_new)
    p = jnp.exp2(s - m_new).astype(BF16)
    return m_new, alpha * acc + _dot(p, vt)


def _init_carry(tq):
    return (jnp.full((tq, 1), NEG_INF, F32), jnp.zeros((tq, LANES), F32))


def _normalised(acc, h):
    l = jnp.sum(jnp.where(_ones_lane_mask(h), acc, 0.0), axis=-1, keepdims=True)
    return jnp.where(_head_lane_mask(h), acc / l, 0.0)


def _fox_kernel(q_ref, k_ref, v_ref, f_ref, o_ref, *, tq, tk):
    hp = pl.program_id(1)
    qi = pl.program_id(2)
    q = q_ref[0]
    row = qi * tq + lax.broadcasted_iota(jnp.int32, (tq, 1), 0)
    n_full = (qi * tq) // tk
    heads = range(HEADS_PER_BLOCK)
    qh = [jnp.where(_head_lane_mask(h), q, jnp.zeros_like(q)) for h in heads]

    def scores(j):
        kt = k_ref[0, pl.ds(pl.multiple_of(j * tk, tk), tk), :]
        return tuple(_nt_dot(qh[h], kt) for h in heads)

    def update(j, ss, carry, diag):
        k0 = pl.multiple_of(j * tk, tk)
        v = v_ref[0, pl.ds(k0, tk), :]
        out = []
        for h in heads:
            s = ss[h] - f_ref[0, pl.ds(hp * HEADS_PER_BLOCK + h, 1), pl.ds(k0, tk)]
            if diag:
                col = k0 + lax.broadcasted_iota(jnp.int32, (1, tk), 1)
                s = jnp.where(col <= row, s, NEG_INF)
            out.append(_online_update(carry[h], s, _values_with_ones(v, h)))
        return tuple(out)

    def body(j, carry):
        return update(j, scores(j), carry, diag=False)

    carry = lax.fori_loop(0, n_full, body, tuple(_init_carry(tq) for _ in heads))
    carry = update(n_full, scores(n_full), carry, diag=True)
    out = _normalised(carry[0][1], 0)
    for h in heads[1:]:
        out = out + _normalised(carry[h][1], h)
    o_ref[0] = out.astype(BF16)


def _fox_attention(z, fcum, *, tq, tk):
    b, s, _ = z.shape
    nblk = W_FOX // LANES
    return pl.pallas_call(
        functools.partial(_fox_kernel, tq=tq, tk=tk),
        grid=(b, nblk, s // tq),
        in_specs=[
            pl.BlockSpec((1, tq, LANES), lambda bi, hp, qi: (bi, qi, ZB_QF + hp)),
            pl.BlockSpec((1, s, LANES), lambda bi, hp, qi: (bi, 0, ZB_KF + hp)),
            pl.BlockSpec((1, s, LANES), lambda bi, hp, qi: (bi, 0, ZB_VF + hp)),
            pl.BlockSpec((1, FG_ROWS, s), lambda bi, hp, qi: (bi, 0, 0)),
        ],
        out_specs=pl.BlockSpec((1, tq, LANES), lambda bi, hp, qi: (bi, qi, hp)),
        out_shape=jax.ShapeDtypeStruct((b, s, W_FOX), BF16),
        compiler_params=pltpu.CompilerParams(
            dimension_semantics=("parallel", "parallel", "arbitrary"), vmem_limit_bytes=VMEM_LIMIT),
        name="fox_attn",
    )(z, z, z, fcum)


def _diff_kernel(q1_ref, q2_ref, k1_ref, k2_ref, v_ref, lp_ref, sg_ref, o_ref, *, tq, tk, lam_init):
    hp = pl.program_id(1)
    qi = pl.program_id(2)
    q1 = q1_ref[0]
    q2 = q2_ref[0]
    row = qi * tq + lax.broadcasted_iota(jnp.int32, (tq, 1), 0)
    n_full = (qi * tq) // tk
    n_all = ((qi + 1) * tq + tk - 1) // tk
    heads = range(HEADS_PER_BLOCK)

    lp = lp_ref[...]
    lam = (jnp.exp(jnp.sum(lp[0:1] * lp[1:2], axis=-1, keepdims=True))
           - jnp.exp(jnp.sum(lp[2:3] * lp[3:4], axis=-1, keepdims=True)) + lam_init)

    q1h = [jnp.where(_head_lane_mask(h), q1, jnp.zeros_like(q1)) for h in heads]
    q2h = [jnp.where(_head_lane_mask(h), q2, jnp.zeros_like(q2)) for h in heads]
    slope = [jnp.where(hp == 0, ALIBI_SLOPES[h] * LOG2E, ALIBI_SLOPES[HEADS_PER_BLOCK + h] * LOG2E)
             .astype(F32) for h in heads]

    def tile(j, carry, diag):
        k0 = pl.multiple_of(j * tk, tk)
        k1t = k1_ref[0, pl.ds(k0, tk), :]
        k2t = k2_ref[0, pl.ds(k0, tk), :]
        v = v_ref[0, pl.ds(k0, tk), :]
        col = k0 + lax.broadcasted_iota(jnp.int32, (1, tk), 1)
        ss1 = [_nt_dot(q1h[h], k1t) for h in heads]
        ss2 = [_nt_dot(q2h[h], k2t) for h in heads]
        out = []
        for h in heads:
            vt = _values_with_ones(v, h)
            s1 = ss1[h]
            s2 = ss2[h]
            if diag:
                bias = slope[h] * jnp.minimum(col, 2 * row - col).astype(F32)
                allowed = (col // CHUNK) <= (row // CHUNK)
                s1 = jnp.where(allowed, s1 + bias, NEG_INF)
                s2 = jnp.where(allowed, s2 + bias, NEG_INF)
            else:
                bias = slope[h] * col.astype(F32)
                s1 = s1 + bias
                s2 = s2 + bias
            c1, c2 = carry[h]
            out.append((_online_update(c1, s1, vt), _online_update(c2, s2, vt)))
        return tuple(out)

    carry = tuple((_init_carry(tq), _init_carry(tq)) for _ in heads)
    carry = lax.fori_loop(0, n_full, functools.partial(tile, diag=False), carry)
    carry = lax.fori_loop(n_full, n_all, functools.partial(tile, diag=True), carry)
    out = jnp.zeros((tq, LANES), F32)
    for h in heads:
        (_, a1), (_, a2) = carry[h]
        out = out + (_normalised(a1, h) - lam * _normalised(a2, h))

    sq = out * out
    inv = jnp.zeros((tq, LANES), F32)
    for h in heads:
        hm = _head_lane_mask(h)
        ms = jnp.sum(jnp.where(hm, sq, 0.0), axis=-1, keepdims=True) * (1.0 / HEAD_DIM)
        inv = jnp.where(hm, lax.rsqrt(ms + EPS), inv)
    o_ref[0] = ((out * inv) * sg_ref[...] * (1.0 - lam_init)).astype(BF16)


def _diff_attention(z, lam_params, subln2, *, tq, tk, lam_init):
    b, s, _ = z.shape
    nblk = W_DIFF // LANES
    return pl.pallas_call(
        functools.partial(_diff_kernel, tq=tq, tk=tk, lam_init=lam_init),
        grid=(b, nblk, s // tq),
        in_specs=[
            pl.BlockSpec((1, tq, LANES), lambda bi, hp, qi: (bi, qi, ZB_Q1 + hp)),
            pl.BlockSpec((1, tq, LANES), lambda bi, hp, qi: (bi, qi, ZB_Q2 + hp)),
            pl.BlockSpec((1, s, LANES), lambda bi, hp, qi: (bi, 0, ZB_K1 + hp)),
            pl.BlockSpec((1, s, LANES), lambda bi, hp, qi: (bi, 0, ZB_K2 + hp)),
            pl.BlockSpec((1, s, LANES), lambda bi, hp, qi: (bi, 0, ZB_VA + hp)),
            pl.BlockSpec((4, HEAD_DIM), lambda bi, hp, qi: (0, 0)),
            pl.BlockSpec((1, LANES), lambda bi, hp, qi: (0, 0)),
        ],
        out_specs=pl.BlockSpec((1, tq, LANES), lambda bi, hp, qi: (bi, qi, hp)),
        out_shape=jax.ShapeDtypeStruct((b, s, W_DIFF), BF16),
        compiler_params=pltpu.CompilerParams(
            dimension_semantics=("parallel", "parallel", "arbitrary"), vmem_limit_bytes=VMEM_LIMIT),
        name="diff_attn",
    )(z, z, z, z, z, lam_params, subln2)


CHUNK_TILE = 256
CHUNK_NTILES = 1 + (N_PREV_CHUNKS * CHUNK) // CHUNK_TILE


def _chunk_kernel(q_ref, k_ref, v_ref, bm_ref, o_ref):
    qi = pl.program_id(2)
    t = CHUNK_TILE
    q = q_ref[0]
    heads = range(HEADS_PER_BLOCK)
    qh = [jnp.where(_head_lane_mask(h), q, jnp.zeros_like(q)) for h in heads]
    k0s = [pl.multiple_of(jnp.maximum(qi - d, 0) * t, t) for d in range(CHUNK_NTILES)]
    raw = [[_nt_dot(qh[h], k_ref[0, pl.ds(k0s[d], t), :]) for d in range(CHUNK_NTILES)] for h in heads]
    out = jnp.zeros((t, LANES), F32)
    for h in heads:
        ss = []
        vs = []
        for d in range(CHUNK_NTILES):
            s = raw[h][d] + bm_ref[h, d]
            ss.append(jnp.where(qi - d >= 0, s, NEG_INF))
            vs.append(_values_with_ones(v_ref[0, pl.ds(k0s[d], t), :], h))
        m = ss[0].max(axis=-1, keepdims=True)
        for s in ss[1:]:
            m = jnp.maximum(m, s.max(axis=-1, keepdims=True))
        acc = jnp.zeros((t, LANES), F32)
        for s, vt in zip(ss, vs):
            acc = acc + _dot(jnp.exp2(s - m).astype(BF16), vt)
        out = out + _normalised(acc, h)
    o_ref[0] = out.astype(BF16)


def _chunk_attention(z, bm):
    b, s, _ = z.shape
    t = CHUNK_TILE
    nblk = W_CHUNK // LANES
    return pl.pallas_call(
        _chunk_kernel,
        grid=(b, nblk, s // t),
        in_specs=[
            pl.BlockSpec((1, t, LANES), lambda bi, hp, qi: (bi, qi, ZB_QC + hp)),
            pl.BlockSpec((1, s, LANES), lambda bi, hp, qi: (bi, 0, ZB_KC + hp)),
            pl.BlockSpec((1, s, LANES), lambda bi, hp, qi: (bi, 0, ZB_VC + hp)),
            pl.BlockSpec((HEADS_PER_BLOCK, CHUNK_NTILES, t, t), lambda bi, hp, qi: (hp, 0, 0, 0)),
        ],
        out_specs=pl.BlockSpec((1, t, LANES), lambda bi, hp, qi: (bi, qi, hp)),
        out_shape=jax.ShapeDtypeStruct((b, s, W_CHUNK), BF16),
        compiler_params=pltpu.CompilerParams(
            dimension_semantics=("parallel", "parallel", "arbitrary"), vmem_limit_bytes=VMEM_LIMIT),
        name="chunk_attn",
    )(z, z, z, bm)


def _chunk_bias_table(rel_table):
    t = CHUNK_TILE
    period = 2 * t
    u = np.arange(period)
    rel = np.where(u < t, -u, period - u)
    idx = np.stack([np.clip(rel + t * d, -REL_CLIP, REL_CLIP) + REL_CLIP for d in range(CHUNK_NTILES)])
    vec = rel_table.astype(F32)[:, idx] * LOG2E
    flat = jnp.tile(vec, (1, 1, t))[..., :t * (period - 1)]
    toep = flat.reshape(vec.shape[0], CHUNK_NTILES, t, period - 1)[..., :t]
    a = np.arange(t)[:, None]
    b = np.arange(t)[None, :]
    valid = []
    for d in range(CHUNK_NTILES):
        qc = a // CHUNK + (t // CHUNK) * d
        kc = b // CHUNK
        valid.append((kc <= qc) & (kc >= qc - N_PREV_CHUNKS))
    return jnp.where(np.stack(valid)[None], toep, NEG_INF)


def _out_proj_kernel(h_ref, oa_ref, ob_ref, oc_ref, w_ref, o_ref, cat_scr):
    cat_scr[:, 0:W_DIFF] = oa_ref[...]
    cat_scr[:, W_DIFF:W_DIFF + W_FOX] = ob_ref[...]
    cat_scr[:, W_DIFF + W_FOX:MIX_WIDTH] = oc_ref[...]
    o_ref[...] = h_ref[...] + _dot(cat_scr[...], w_ref[...])


def _out_proj(h, oa, ob, oc, w, *, tm):
    t = h.shape[0]
    return pl.pallas_call(
        _out_proj_kernel,
        grid=(t // tm,),
        in_specs=[
            pl.BlockSpec((tm, D_MODEL), lambda i: (i, 0)),
            pl.BlockSpec((tm, W_DIFF), lambda i: (i, 0)),
            pl.BlockSpec((tm, W_FOX), lambda i: (i, 0)),
            pl.BlockSpec((tm, W_CHUNK), lambda i: (i, 0)),
            pl.BlockSpec((MIX_WIDTH, D_MODEL), lambda i: (0, 0)),
        ],
        out_specs=pl.BlockSpec((tm, D_MODEL), lambda i: (i, 0)),
        out_shape=jax.ShapeDtypeStruct((t, D_MODEL), F32),
        scratch_shapes=[pltpu.VMEM((tm, MIX_WIDTH), BF16)],
        compiler_params=pltpu.CompilerParams(
            dimension_semantics=("parallel",), vmem_limit_bytes=VMEM_LIMIT),
        name="out_proj",
    )(h, oa, ob, oc, w)


MLP_CW = 256
HALO = 8


def _mlp_kernel(h_ref, g_ref, wup_ref, cw_ref, cb_ref, wdn_ref, o_ref,
                hn_scr, ubuf, tail, acc, *, tiles_per_batch):
    i = pl.program_id(0)
    tm = h_ref.shape[0]

    @pl.when(i % tiles_per_batch == 0)
    def _():
        tail[...] = jnp.zeros_like(tail)

    hn_scr[...] = _rms_rows(h_ref[...], g_ref[...]).astype(BF16)

    def up(c):
        g0 = c * MLP_CW
        v0 = D_FF + c * MLP_CW
        return (_dot(hn_scr[...], wup_ref[:, g0:g0 + MLP_CW]), _dot(hn_scr[...], wup_ref[:, v0:v0 + MLP_CW]))

    def conv(u, c0, slot):
        cols = slice(c0, c0 + MLP_CW)
        ubuf[slot, 0:HALO, :] = tail[:, cols]
        ubuf[slot, HALO:HALO + tm, :] = u
        tail[:, cols] = u[tm - HALO:tm, :]
        return (cb_ref[:, cols]
                + cw_ref[0:1, cols] * ubuf[slot, HALO - 2:HALO - 2 + tm, :]
                + cw_ref[1:2, cols] * ubuf[slot, HALO - 1:HALO - 1 + tm, :]
                + cw_ref[2:3, cols] * u)

    n_chunks = D_FF // MLP_CW
    u_next = up(0)
    for c in range(n_chunks):
        ug, uv = u_next
        if c + 1 < n_chunks:
            u_next = up(c + 1)
        gate = conv(ug, c * MLP_CW, 0)
        val = conv(uv, D_FF + c * MLP_CW, 1)
        act = (gate * jax.nn.sigmoid(gate) * val).astype(BF16)
        part = _dot(act, wdn_ref[c * MLP_CW:(c + 1) * MLP_CW, :])
        if c == 0:
            acc[...] = part
        else:
            acc[...] += part
    o_ref[...] = h_ref[...] + acc[...]


def _mlp(h, g, wup, cw, cb, wdn, *, seq, tm):
    t = h.shape[0]
    const = lambda i: (0, 0)
    return pl.pallas_call(
        functools.partial(_mlp_kernel, tiles_per_batch=seq // tm),
        grid=(t // tm,),
        in_specs=[
            pl.BlockSpec((tm, D_MODEL), lambda i: (i, 0)),
            pl.BlockSpec((1, D_MODEL), const),
            pl.BlockSpec((D_MODEL, 2 * D_FF), const, pipeline_mode=pl.Buffered(1)),
            pl.BlockSpec((CONV_WIDTH, 2 * D_FF), const),
            pl.BlockSpec((1, 2 * D_FF), const),
            pl.BlockSpec((D_FF, D_MODEL), const, pipeline_mode=pl.Buffered(1)),
        ],
        out_specs=pl.BlockSpec((tm, D_MODEL), lambda i: (i, 0)),
        out_shape=jax.ShapeDtypeStruct((t, D_MODEL), F32),
        scratch_shapes=[
            pltpu.VMEM((tm, D_MODEL), BF16),
            pltpu.VMEM((2, HALO + tm, MLP_CW), F32),
            pltpu.VMEM((HALO, 2 * D_FF), F32),
            pltpu.VMEM((tm, D_MODEL), F32),
        ],
        compiler_params=pltpu.CompilerParams(
            dimension_semantics=("arbitrary",), vmem_limit_bytes=VMEM_LIMIT),
        name="conv_mlp",
    )(h, g, wup, cw, cb, wdn)


def _ple_kernel(h_ref, p_ref, g_ref, wg_ref, wp_ref, o_ref):
    h = h_ref[...]
    hn = _rms_rows(h, g_ref[...]).astype(BF16)
    gate = jax.nn.sigmoid(_dot(hn, wg_ref[...]))
    proj = _dot(p_ref[...].astype(BF16), wp_ref[...])
    o_ref[...] = h + proj * gate


def _ple(h, p, g, wg, wp, *, tm):
    t = h.shape[0]
    const = lambda i: (0, 0)
    return pl.pallas_call(
        _ple_kernel,
        grid=(t // tm,),
        in_specs=[
            pl.BlockSpec((tm, D_MODEL), lambda i: (i, 0)),
            pl.BlockSpec((tm, PLE_DIM), lambda i: (i, 0)),
            pl.BlockSpec((1, D_MODEL), const),
            pl.BlockSpec((D_MODEL, D_MODEL), const),
            pl.BlockSpec((PLE_DIM, D_MODEL), const),
        ],
        out_specs=pl.BlockSpec((tm, D_MODEL), lambda i: (i, 0)),
        out_shape=jax.ShapeDtypeStruct((t, D_MODEL), F32),
        compiler_params=pltpu.CompilerParams(
            dimension_semantics=("parallel",), vmem_limit_bytes=VMEM_LIMIT),
        name="ple",
    )(h, p, g, wg, wp)


def _prep_w_in(w_in, qk_gain):
    sizes = (W_DIFF,) * 5 + (W_FOX,) * 3 + (H_FOX,) + (W_CHUNK,) * 3
    offs = np.cumsum((0,) + sizes)
    seg = lambda n: w_in[:, offs[n]:offs[n + 1]]
    q1, q2, k1, k2, va, qf, kf, vf, fg, qc, kc, vc = (seg(n) for n in range(12))
    w = jnp.concatenate([q1, q2, k1, k2, qf, kf, qc, kc, va, vf, vc], axis=1).astype(BF16)
    wfg = jnp.zeros((FG_ROWS, D_MODEL), F32).at[:H_FOX].set(fg.T).astype(BF16)
    scale = HEAD_DIM ** -0.5 * LOG2E
    g = qk_gain.astype(F32)
    gains = jnp.concatenate([
        jnp.tile(g[0] * scale, H_DIFF), jnp.tile(g[0] * scale, H_DIFF),
        jnp.tile(g[1], H_DIFF), jnp.tile(g[1], H_DIFF),
        jnp.tile(g[2] * scale, H_FOX), jnp.tile(g[3], H_FOX),
        jnp.tile(g[4] * scale, H_CHUNK), jnp.tile(g[5], H_CHUNK),
        jnp.ones((Z_COLS - Z_NORM_COLS,), F32)])[None, :]
    return w, wfg, gains


def _block_diag_ones(width):
    r = np.arange(width)
    return jnp.asarray((r[:, None] // HEAD_DIM) == (r[None, :] // HEAD_DIM), BF16)


def _pick(n, pref):
    while n % pref:
        pref //= 2
    return pref


def kernel(x, p, ln_mix, w_in, qk_gain, lam_params, subln_gain, fgate_bias, rel_bias, w_out,
           ln_ffn, w_up, conv_w, conv_b, w_down, ln_ple, w_ple_gate, w_ple_proj):
    b, s, d = x.shape
    depth = w_in.shape[0]
    t = b * s
    tm_proj = _pick(s, 512)
    tn_proj = 512
    tm_mlp = _pick(s, 512)
    tm_row = _pick(s, 512)
    tq = _pick(s, 512)
    tk = _pick(s, 512)

    bd = _block_diag_ones(256)
    tri = jnp.asarray(np.triu(np.ones((tm_proj, tm_proj), np.float32)), BF16)

    h = x.reshape(t, d).astype(F32)
    for i in range(depth):
        lam_init = 0.8 - 0.6 * math.exp(-0.3 * i)
        w, wfg, gains = _prep_w_in(w_in[i], qk_gain[i])
        fb = jnp.zeros((FG_ROWS, LANES), F32).at[:H_FOX, :].set(
            jnp.broadcast_to(fgate_bias[i].astype(F32)[:, None], (H_FOX, LANES)))
        z, fcum = _proj_in(h, ln_mix[i][None, :].astype(F32), w, wfg, fb, gains, bd, tri,
                           batch=b, seq=s, tm=tm_proj, tn=tn_proj)
        z = z.reshape(b, s, Z_COLS)
        subln2 = jnp.tile(subln_gain[i].astype(F32), HEADS_PER_BLOCK)[None, :]
        oa = _diff_attention(z, lam_params[i].astype(F32), subln2, tq=tq, tk=tk, lam_init=lam_init)
        ob = _fox_attention(z, fcum, tq=tq, tk=tk)
        oc = _chunk_attention(z, _chunk_bias_table(rel_bias[i]))
        h = _out_proj(h, oa.reshape(t, W_DIFF), ob.reshape(t, W_FOX), oc.reshape(t, W_CHUNK),
                      w_out[i].astype(BF16), tm=tm_row)
        h = _mlp(h, ln_ffn[i][None, :].astype(F32), w_up[i].astype(BF16), conv_w[i].astype(F32),
                 conv_b[i][None, :].astype(F32), w_down[i].astype(BF16), seq=s, tm=tm_mlp)
        h = _ple(h, p[i].reshape(t, PLE_DIM), ln_ple[i][None, :].astype(F32),
                 w_ple_gate[i].astype(BF16), w_ple_proj[i].astype(BF16), tm=tm_row)
    return h.reshape(b, s, d).astype(x.dtype)
```

```python
import functools
import math

import numpy as np
import jax
import jax.numpy as jnp
from jax import lax
from jax.experimental import pallas as pl
from jax.experimental.pallas import tpu as pltpu

F32 = jnp.float32
BF16 = jnp.bfloat16

D_MODEL = 1024
CHUNK = 64
HEAD_DIM = 64
H_DIFF = 4
H_FOX = 6
H_CHUNK = 6
N_PREV_CHUNKS = 8
REL_CLIP = 128
D_FF = 2816
CONV_WIDTH = 3
PLE_DIM = 256
EPS = 1e-6
NEG_INF = -1e30
LOG2E = math.log2(math.e)

W_DIFF = H_DIFF * HEAD_DIM
W_FOX = H_FOX * HEAD_DIM
W_CHUNK = H_CHUNK * HEAD_DIM
MIX_WIDTH = W_DIFF + W_FOX + W_CHUNK
ALIBI_SLOPES = tuple(2.0 ** (-8.0 * (h + 1) / H_DIFF) for h in range(H_DIFF))

LANES = 128
HEADS_PER_BLOCK = LANES // HEAD_DIM
FG_ROWS = 8

_Z_SIZES = (W_DIFF,) * 4 + (W_FOX,) * 2 + (W_CHUNK,) * 2 + (W_DIFF, W_FOX, W_CHUNK)
_Z_OFFS = tuple(int(c) for c in np.cumsum((0,) + _Z_SIZES))
Z_COLS = _Z_OFFS[-1]
Z_NORM_COLS = _Z_OFFS[8]
(ZB_Q1, ZB_Q2, ZB_K1, ZB_K2, ZB_QF, ZB_KF, ZB_QC, ZB_KC, ZB_VA, ZB_VF, ZB_VC) = (
    o // LANES for o in _Z_OFFS[:-1])

VMEM_LIMIT = 56 * 1024 * 1024


def _nt_dot(a, b):
    return lax.dot_general(a, b, (((1,), (1,)), ((), ())), preferred_element_type=F32)


def _dot(a, b):
    return jnp.dot(a, b, preferred_element_type=F32)


def _rms_rows(x, g):
    ms = jnp.mean(x * x, axis=-1, keepdims=True)
    return (x * lax.rsqrt(ms + EPS)) * g


def _proj_in_kernel(h_ref, g_ref, w_ref, wfg_ref, fb_ref, gain_ref, bd_ref, tri_ref,
                    z_ref, f_ref, carry_scr, *, tiles_per_batch, tn):
    i = pl.program_id(0)
    tm = h_ref.shape[0]
    hnb = _rms_rows(h_ref[...], g_ref[...]).astype(BF16)

    x = _nt_dot(wfg_ref[...], hnb) + fb_ref[:, 0:1]
    logf = jnp.minimum(x, 0.0) - jnp.log1p(jnp.exp(-jnp.abs(x)))
    hi = logf.astype(BF16)
    r1 = logf - hi.astype(F32)
    mid = r1.astype(BF16)
    lo = (r1 - mid.astype(F32)).astype(BF16)
    tri = tri_ref[...]
    cs = _dot(hi, tri) + _dot(mid, tri) + _dot(lo, tri)

    @pl.when(i % tiles_per_batch == 0)
    def _():
        carry_scr[...] = jnp.zeros_like(carry_scr)

    fcum = cs + carry_scr[:, 0:1]
    f_ref[0] = fcum * LOG2E
    carry_scr[...] = jnp.broadcast_to(fcum[:, tm - 1:tm], carry_scr.shape)

    bd = bd_ref[...]
    w = bd.shape[0]
    n_blocks = Z_COLS // tn
    project = lambda c: _dot(hnb, w_ref[:, c * tn:(c + 1) * tn])
    z_next = project(0)
    for c in range(n_blocks):
        z = z_next
        if c + 1 < n_blocks:
            z_next = project(c + 1)
        if (c + 1) * tn <= Z_NORM_COLS:
            for s in range(tn // w):
                cols = slice(c * tn + s * w, c * tn + (s + 1) * w)
                zz = z[:, s * w:(s + 1) * w]
                ss = _dot((zz * zz).astype(BF16), bd)
                inv = lax.rsqrt(ss * (1.0 / HEAD_DIM) + EPS)
                z_ref[:, cols] = (zz * inv * gain_ref[:, cols]).astype(BF16)
        else:
            z_ref[:, c * tn:(c + 1) * tn] = z.astype(BF16)


def _proj_in(h, g, w, wfg, fb, gain, bd, tri, *, batch, seq, tm, tn):
    t = h.shape[0]
    tiles_per_batch = seq // tm
    assert Z_NORM_COLS % tn == 0 and Z_COLS % tn == 0 and tn % bd.shape[0] == 0
    kern = functools.partial(_proj_in_kernel, tiles_per_batch=tiles_per_batch, tn=tn)
    const = lambda i: (0, 0)
    return pl.pallas_call(
        kern,
        grid=(t // tm,),
        in_specs=[
            pl.BlockSpec((tm, D_MODEL), lambda i: (i, 0)),
            pl.BlockSpec((1, D_MODEL), const),
            pl.BlockSpec((D_MODEL, Z_COLS), const, pipeline_mode=pl.Buffered(1)),
            pl.BlockSpec((FG_ROWS, D_MODEL), const),
            pl.BlockSpec((FG_ROWS, LANES), const),
            pl.BlockSpec((1, Z_COLS), const),
            pl.BlockSpec(bd.shape, const),
            pl.BlockSpec((tm, tm), const),
        ],
        out_specs=[
            pl.BlockSpec((tm, Z_COLS), lambda i: (i, 0)),
            pl.BlockSpec((1, FG_ROWS, tm), lambda i: (i // tiles_per_batch, 0, i % tiles_per_batch)),
        ],
        out_shape=[
            jax.ShapeDtypeStruct((t, Z_COLS), BF16),
            jax.ShapeDtypeStruct((batch, FG_ROWS, seq), F32),
        ],
        scratch_shapes=[pltpu.VMEM((FG_ROWS, LANES), F32)],
        compiler_params=pltpu.CompilerParams(
            dimension_semantics=("arbitrary",), vmem_limit_bytes=VMEM_LIMIT),
        name="proj_in",
    )(h, g, w, wfg, fb, gain, bd, tri)


def _head_lane_mask(h):
    lane = lax.broadcasted_iota(jnp.int32, (1, LANES), 1)
    return (lane >= h * HEAD_DIM) & (lane < (h + 1) * HEAD_DIM)


def _ones_lane_mask(h):
    lane = lax.broadcasted_iota(jnp.int32, (1, LANES), 1)
    return lane == ((h + 1) % HEADS_PER_BLOCK) * HEAD_DIM


def _values_with_ones(v, h):
    fill = jnp.where(_ones_lane_mask(h), 1.0, 0.0).astype(v.dtype)
    return jnp.where(_head_lane_mask(h), v, jnp.broadcast_to(fill, v.shape))


def _online_update(carry, s, vt):
    m, acc = carry
    m_new = jnp.maximum(m, jnp.max(s, axis=-1, keepdims=True))
    alpha = jnp.exp2(m - m_new)
    p = jnp.exp2(s - m_new).astype(BF16)
    return m_new, alpha * acc + _dot(p, vt)


def _init_carry(tq):
    return (jnp.full((tq, 1), NEG_INF, F32), jnp.zeros((tq, LANES), F32))


def _normalised(acc, h):
    l = jnp.sum(jnp.where(_ones_lane_mask(h), acc, 0.0), axis=-1, keepdims=True)
    return jnp.where(_head_lane_mask(h), acc / l, 0.0)


def _fox_kernel(q_ref, k_ref, v_ref, f_ref, o_ref, *, tq, tk):
    hp = pl.program_id(1)
    qi = pl.program_id(2)
    q = q_ref[0]
    row = qi * tq + lax.broadcasted_iota(jnp.int32, (tq, 1), 0)
    n_full = (qi * tq) // tk
    heads = range(HEADS_PER_BLOCK)
    qh = [jnp.where(_head_lane_mask(h), q, jnp.zeros_like(q)) for h in heads]

    def scores(j):
        kt = k_ref[0, pl.ds(pl.multiple_of(j * tk, tk), tk), :]
        return tuple(_nt_dot(qh[h], kt) for h in heads)

    def update(j, ss, carry, diag):
        k0 = pl.multiple_of(j * tk, tk)
        v = v_ref[0, pl.ds(k0, tk), :]
        out = []
        for h in heads:
            s = ss[h] - f_ref[0, pl.ds(hp * HEADS_PER_BLOCK + h, 1), pl.ds(k0, tk)]
            if diag:
                col = k0 + lax.broadcasted_iota(jnp.int32, (1, tk), 1)
                s = jnp.where(col <= row, s, NEG_INF)
            out.append(_online_update(carry[h], s, _values_with_ones(v, h)))
        return tuple(out)

    def body(j, carry):
        return update(j, scores(j), carry, diag=False)

    carry = lax.fori_loop(0, n_full, body, tuple(_init_carry(tq) for _ in heads))
    carry = update(n_full, scores(n_full), carry, diag=True)
    out = _normalised(carry[0][1], 0)
    for h in heads[1:]:
        out = out + _normalised(carry[h][1], h)
    o_ref[0] = out.astype(BF16)


def _fox_attention(z, fcum, *, tq, tk):
    b, s, _ = z.shape
    nblk = W_FOX // LANES
    return pl.pallas_call(
        functools.partial(_fox_kernel, tq=tq, tk=tk),
        grid=(b, nblk, s // tq),
        in_specs=[
            pl.BlockSpec((1, tq, LANES), lambda bi, hp, qi: (bi, qi, ZB_QF + hp)),
            pl.BlockSpec((1, s, LANES), lambda bi, hp, qi: (bi, 0, ZB_KF + hp)),
            pl.BlockSpec((1, s, LANES), lambda bi, hp, qi: (bi, 0, ZB_VF + hp)),
            pl.BlockSpec((1, FG_ROWS, s), lambda bi, hp, qi: (bi, 0, 0)),
        ],
        out_specs=pl.BlockSpec((1, tq, LANES), lambda bi, hp, qi: (bi, qi, hp)),
        out_shape=jax.ShapeDtypeStruct((b, s, W_FOX), BF16),
        compiler_params=pltpu.CompilerParams(
            dimension_semantics=("parallel", "parallel", "arbitrary"), vmem_limit_bytes=VMEM_LIMIT),
        name="fox_attn",
    )(z, z, z, fcum)


def _diff_kernel(q1_ref, q2_ref, k1_ref, k2_ref, v_ref, lp_ref, sg_ref, o_ref, *, tq, tk, lam_init):
    hp = pl.program_id(1)
    qi = pl.program_id(2)
    q1 = q1_ref[0]
    q2 = q2_ref[0]
    row = qi * tq + lax.broadcasted_iota(jnp.int32, (tq, 1), 0)
    n_full = (qi * tq) // tk
    n_all = ((qi + 1) * tq + tk - 1) // tk
    heads = range(HEADS_PER_BLOCK)

    lp = lp_ref[...]
    lam = (jnp.exp(jnp.sum(lp[0:1] * lp[1:2], axis=-1, keepdims=True))
           - jnp.exp(jnp.sum(lp[2:3] * lp[3:4], axis=-1, keepdims=True)) + lam_init)

    q1h = [jnp.where(_head_lane_mask(h), q1, jnp.zeros_like(q1)) for h in heads]
    q2h = [jnp.where(_head_lane_mask(h), q2, jnp.zeros_like(q2)) for h in heads]
    slope = [jnp.where(hp == 0, ALIBI_SLOPES[h] * LOG2E, ALIBI_SLOPES[HEADS_PER_BLOCK + h] * LOG2E)
             .astype(F32) for h in heads]

    def tile(j, carry, diag):
        k0 = pl.multiple_of(j * tk, tk)
        k1t = k1_ref[0, pl.ds(k0, tk), :]
        k2t = k2_ref[0, pl.ds(k0, tk), :]
        v = v_ref[0, pl.ds(k0, tk), :]
        col = k0 + lax.broadcasted_iota(jnp.int32, (1, tk), 1)
        ss1 = [_nt_dot(q1h[h], k1t) for h in heads]
        ss2 = [_nt_dot(q2h[h], k2t) for h in heads]
        out = []
        for h in heads:
            vt = _values_with_ones(v, h)
            s1 = ss1[h]
            s2 = ss2[h]
            if diag:
                bias = slope[h] * jnp.minimum(col, 2 * row - col).astype(F32)
                allowed = (col // CHUNK) <= (row // CHUNK)
                s1 = jnp.where(allowed, s1 + bias, NEG_INF)
                s2 = jnp.where(allowed, s2 + bias, NEG_INF)
            else:
                bias = slope[h] * col.astype(F32)
                s1 = s1 + bias
                s2 = s2 + bias
            c1, c2 = carry[h]
            out.append((_online_update(c1, s1, vt), _online_update(c2, s2, vt)))
        return tuple(out)

    carry = tuple((_init_carry(tq), _init_carry(tq)) for _ in heads)
    carry = lax.fori_loop(0, n_full, functools.partial(tile, diag=False), carry)
    carry = lax.fori_loop(n_full, n_all, functools.partial(tile, diag=True), carry)
    out = jnp.zeros((tq, LANES), F32)
    for h in heads:
        (_, a1), (_, a2) = carry[h]
        out = out + (_normalised(a1, h) - lam * _normalised(a2, h))

    sq = out * out
    inv = jnp.zeros((tq, LANES), F32)
    for h in heads:
        hm = _head_lane_mask(h)
        ms = jnp.sum(jnp.where(hm, sq, 0.0), axis=-1, keepdims=True) * (1.0 / HEAD_DIM)
        inv = jnp.where(hm, lax.rsqrt(ms + EPS), inv)
    o_ref[0] = ((out * inv) * sg_ref[...] * (1.0 - lam_init)).astype(BF16)


def _diff_attention(z, lam_params, subln2, *, tq, tk, lam_init):
    b, s, _ = z.shape
    nblk = W_DIFF // LANES
    return pl.pallas_call(
        functools.partial(_diff_kernel, tq=tq, tk=tk, lam_init=lam_init),
        grid=(b, nblk, s // tq),
        in_specs=[
            pl.BlockSpec((1, tq, LANES), lambda bi, hp, qi: (bi, qi, ZB_Q1 + hp)),
            pl.BlockSpec((1, tq, LANES), lambda bi, hp, qi: (bi, qi, ZB_Q2 + hp)),
            pl.BlockSpec((1, s, LANES), lambda bi, hp, qi: (bi, 0, ZB_K1 + hp)),
            pl.BlockSpec((1, s, LANES), lambda bi, hp, qi: (bi, 0, ZB_K2 + hp)),
            pl.BlockSpec((1, s, LANES), lambda bi, hp, qi: (bi, 0, ZB_VA + hp)),
            pl.BlockSpec((4, HEAD_DIM), lambda bi, hp, qi: (0, 0)),
            pl.BlockSpec((1, LANES), lambda bi, hp, qi: (0, 0)),
        ],
        out_specs=pl.BlockSpec((1, tq, LANES), lambda bi, hp, qi: (bi, qi, hp)),
        out_shape=jax.ShapeDtypeStruct((b, s, W_DIFF), BF16),
        compiler_params=pltpu.CompilerParams(
            dimension_semantics=("parallel", "parallel", "arbitrary"), vmem_limit_bytes=VMEM_LIMIT),
        name="diff_attn",
    )(z, z, z, z, z, lam_params, subln2)


CHUNK_TILE = 256
CHUNK_NTILES = 1 + (N_PREV_CHUNKS * CHUNK) // CHUNK_TILE
CHUNK_QTILES = 4


def _chunk_kernel(q_ref, k_ref, v_ref, bm_ref, o_ref):
    t = CHUNK_TILE
    heads = range(HEADS_PER_BLOCK)
    subs = range(CHUNK_QTILES)
    tiles = range(CHUNK_NTILES)
    qts = [pl.program_id(2) * CHUNK_QTILES + u for u in subs]
    k0s = [[pl.multiple_of(jnp.maximum(qts[u] - d, 0) * t, t) for d in tiles] for u in subs]
    raw = []
    for u in subs:
        q = q_ref[0, u * t:(u + 1) * t, :]
        qh = [jnp.where(_head_lane_mask(h), q, jnp.zeros_like(q)) for h in heads]
        raw.append([[_nt_dot(qh[h], k_ref[0, pl.ds(k0s[u][d], t), :]) for d in tiles] for h in heads])
    for u in subs:
        out = jnp.zeros((t, LANES), F32)
        for h in heads:
            ss = []
            vs = []
            for d in tiles:
                s = raw[u][h][d] + bm_ref[h, d]
                ss.append(jnp.where(qts[u] - d >= 0, s, NEG_INF))
                vs.append(_values_with_ones(v_ref[0, pl.ds(k0s[u][d], t), :], h))
            m = ss[0].max(axis=-1, keepdims=True)
            for s in ss[1:]:
                m = jnp.maximum(m, s.max(axis=-1, keepdims=True))
            acc = jnp.zeros((t, LANES), F32)
            for s, vt in zip(ss, vs):
                acc = acc + _dot(jnp.exp2(s - m).astype(BF16), vt)
            out = out + _normalised(acc, h)
        o_ref[0, u * t:(u + 1) * t, :] = out.astype(BF16)


def _chunk_attention(z, bm):
    b, s, _ = z.shape
    t = CHUNK_TILE
    tq = CHUNK_TILE * CHUNK_QTILES
    nblk = W_CHUNK // LANES
    return pl.pallas_call(
        _chunk_kernel,
        grid=(b, nblk, s // tq),
        in_specs=[
            pl.BlockSpec((1, tq, LANES), lambda bi, hp, qi: (bi, qi, ZB_QC + hp)),
            pl.BlockSpec((1, s, LANES), lambda bi, hp, qi: (bi, 0, ZB_KC + hp)),
            pl.BlockSpec((1, s, LANES), lambda bi, hp, qi: (bi, 0, ZB_VC + hp)),
            pl.BlockSpec((HEADS_PER_BLOCK, CHUNK_NTILES, t, t), lambda bi, hp, qi: (hp, 0, 0, 0)),
        ],
        out_specs=pl.BlockSpec((1, tq, LANES), lambda bi, hp, qi: (bi, qi, hp)),
        out_shape=jax.ShapeDtypeStruct((b, s, W_CHUNK), BF16),
        compiler_params=pltpu.CompilerParams(
            dimension_semantics=("parallel", "parallel", "arbitrary"), vmem_limit_bytes=VMEM_LIMIT),
        name="chunk_attn",
    )(z, z, z, bm)


def _chunk_bias_table(rel_table):
    t = CHUNK_TILE
    period = 2 * t
    u = np.arange(period)
    rel = np.where(u < t, -u, period - u)
    idx = np.stack([np.clip(rel + t * d, -REL_CLIP, REL_CLIP) + REL_CLIP for d in range(CHUNK_NTILES)])
    vec = rel_table.astype(F32)[:, idx] * LOG2E
    flat = jnp.tile(vec, (1, 1, t))[..., :t * (period - 1)]
    toep = flat.reshape(vec.shape[0], CHUNK_NTILES, t, period - 1)[..., :t]
    a = np.arange(t)[:, None]
    b = np.arange(t)[None, :]
    valid = []
    for d in range(CHUNK_NTILES):
        qc = a // CHUNK + (t // CHUNK) * d
        kc = b // CHUNK
        valid.append((kc <= qc) & (kc >= qc - N_PREV_CHUNKS))
    return jnp.where(np.stack(valid)[None], toep, NEG_INF)


def _out_proj_kernel(h_ref, oa_ref, ob_ref, oc_ref, w_ref, o_ref, cat_scr):
    cat_scr[:, 0:W_DIFF] = oa_ref[...]
    cat_scr[:, W_DIFF:W_DIFF + W_FOX] = ob_ref[...]
    cat_scr[:, W_DIFF + W_FOX:MIX_WIDTH] = oc_ref[...]
    o_ref[...] = h_ref[...] + _dot(cat_scr[...], w_ref[...])


def _out_proj(h, oa, ob, oc, w, *, tm):
    t = h.shape[0]
    return pl.pallas_call(
        _out_proj_kernel,
        grid=(t // tm,),
        in_specs=[
            pl.BlockSpec((tm, D_MODEL), lambda i: (i, 0)),
            pl.BlockSpec((tm, W_DIFF), lambda i: (i, 0)),
            pl.BlockSpec((tm, W_FOX), lambda i: (i, 0)),
            pl.BlockSpec((tm, W_CHUNK), lambda i: (i, 0)),
            pl.BlockSpec((MIX_WIDTH, D_MODEL), lambda i: (0, 0)),
        ],
        out_specs=pl.BlockSpec((tm, D_MODEL), lambda i: (i, 0)),
        out_shape=jax.ShapeDtypeStruct((t, D_MODEL), F32),
        scratch_shapes=[pltpu.VMEM((tm, MIX_WIDTH), BF16)],
        compiler_params=pltpu.CompilerParams(
            dimension_semantics=("parallel",), vmem_limit_bytes=VMEM_LIMIT),
        name="out_proj",
    )(h, oa, ob, oc, w)


MLP_CW = 256
HALO = 8


def _mlp_kernel(h_ref, g_ref, wup_ref, cw_ref, cb_ref, wdn_ref, o_ref,
                hn_scr, ubuf, tail, acc, *, tiles_per_batch):
    i = pl.program_id(0)
    tm = h_ref.shape[0]

    @pl.when(i % tiles_per_batch == 0)
    def _():
        tail[...] = jnp.zeros_like(tail)

    hn_scr[...] = _rms_rows(h_ref[...], g_ref[...]).astype(BF16)

    def up(c):
        g0 = c * MLP_CW
        v0 = D_FF + c * MLP_CW
        return (_dot(hn_scr[...], wup_ref[:, g0:g0 + MLP_CW]), _dot(hn_scr[...], wup_ref[:, v0:v0 + MLP_CW]))

    def conv(u, c0, slot):
        cols = slice(c0, c0 + MLP_CW)
        ubuf[slot, 0:HALO, :] = tail[:, cols]
        ubuf[slot, HALO:HALO + tm, :] = u
        tail[:, cols] = u[tm - HALO:tm, :]
        return (cb_ref[:, cols]
                + cw_ref[0:1, cols] * ubuf[slot, HALO - 2:HALO - 2 + tm, :]
                + cw_ref[1:2, cols] * ubuf[slot, HALO - 1:HALO - 1 + tm, :]
                + cw_ref[2:3, cols] * u)

    n_chunks = D_FF // MLP_CW
    u_next = up(0)
    for c in range(n_chunks):
        ug, uv = u_next
        if c + 1 < n_chunks:
            u_next = up(c + 1)
        gate = conv(ug, c * MLP_CW, 0)
        val = conv(uv, D_FF + c * MLP_CW, 1)
        act = (gate * jax.nn.sigmoid(gate) * val).astype(BF16)
        part = _dot(act, wdn_ref[c * MLP_CW:(c + 1) * MLP_CW, :])
        if c == 0:
            acc[...] = part
        else:
            acc[...] += part
    o_ref[...] = h_ref[...] + acc[...]


def _mlp(h, g, wup, cw, cb, wdn, *, seq, tm):
    t = h.shape[0]
    const = lambda i: (0, 0)
    return pl.pallas_call(
        functools.partial(_mlp_kernel, tiles_per_batch=seq // tm),
        grid=(t // tm,),
        in_specs=[
            pl.BlockSpec((tm, D_MODEL), lambda i: (i, 0)),
            pl.BlockSpec((1, D_MODEL), const),
            pl.BlockSpec((D_MODEL, 2 * D_FF), const, pipeline_mode=pl.Buffered(1)),
            pl.BlockSpec((CONV_WIDTH, 2 * D_FF), const),
            pl.BlockSpec((1, 2 * D_FF), const),
            pl.BlockSpec((D_FF, D_MODEL), const, pipeline_mode=pl.Buffered(1)),
        ],
        out_specs=pl.BlockSpec((tm, D_MODEL), lambda i: (i, 0)),
        out_shape=jax.ShapeDtypeStruct((t, D_MODEL), F32),
        scratch_shapes=[
            pltpu.VMEM((tm, D_MODEL), BF16),
            pltpu.VMEM((2, HALO + tm, MLP_CW), F32),
            pltpu.VMEM((HALO, 2 * D_FF), F32),
            pltpu.VMEM((tm, D_MODEL), F32),
        ],
        compiler_params=pltpu.CompilerParams(
            dimension_semantics=("arbitrary",), vmem_limit_bytes=VMEM_LIMIT),
        name="conv_mlp",
    )(h, g, wup, cw, cb, wdn)


def _ple_kernel(h_ref, p_ref, g_ref, wg_ref, wp_ref, o_ref):
    h = h_ref[...]
    hn = _rms_rows(h, g_ref[...]).astype(BF16)
    gate = jax.nn.sigmoid(_dot(hn, wg_ref[...]))
    proj = _dot(p_ref[...].astype(BF16), wp_ref[...])
    o_ref[...] = h + proj * gate


def _ple(h, p, g, wg, wp, *, tm):
    t = h.shape[0]
    const = lambda i: (0, 0)
    return pl.pallas_call(
        _ple_kernel,
        grid=(t // tm,),
        in_specs=[
            pl.BlockSpec((tm, D_MODEL), lambda i: (i, 0)),
            pl.BlockSpec((tm, PLE_DIM), lambda i: (i, 0)),
            pl.BlockSpec((1, D_MODEL), const),
            pl.BlockSpec((D_MODEL, D_MODEL), const),
            pl.BlockSpec((PLE_DIM, D_MODEL), const),
        ],
        out_specs=pl.BlockSpec((tm, D_MODEL), lambda i: (i, 0)),
        out_shape=jax.ShapeDtypeStruct((t, D_MODEL), F32),
        compiler_params=pltpu.CompilerParams(
            dimension_semantics=("parallel",), vmem_limit_bytes=VMEM_LIMIT),
        name="ple",
    )(h, p, g, wg, wp)


def _prep_w_in(w_in, qk_gain):
    sizes = (W_DIFF,) * 5 + (W_FOX,) * 3 + (H_FOX,) + (W_CHUNK,) * 3
    offs = np.cumsum((0,) + sizes)
    seg = lambda n: w_in[:, offs[n]:offs[n + 1]]
    q1, q2, k1, k2, va, qf, kf, vf, fg, qc, kc, vc = (seg(n) for n in range(12))
    w = jnp.concatenate([q1, q2, k1, k2, qf, kf, qc, kc, va, vf, vc], axis=1).astype(BF16)
    wfg = jnp.zeros((FG_ROWS, D_MODEL), F32).at[:H_FOX].set(fg.T).astype(BF16)
    scale = HEAD_DIM ** -0.5 * LOG2E
    g = qk_gain.astype(F32)
    gains = jnp.concatenate([
        jnp.tile(g[0] * scale, H_DIFF), jnp.tile(g[0] * scale, H_DIFF),
        jnp.tile(g[1], H_DIFF), jnp.tile(g[1], H_DIFF),
        jnp.tile(g[2] * scale, H_FOX), jnp.tile(g[3], H_FOX),
        jnp.tile(g[4] * scale, H_CHUNK), jnp.tile(g[5], H_CHUNK),
        jnp.ones((Z_COLS - Z_NORM_COLS,), F32)])[None, :]
    return w, wfg, gains


def _block_diag_ones(width):
    r = np.arange(width)
    return jnp.asarray((r[:, None] // HEAD_DIM) == (r[None, :] // HEAD_DIM), BF16)


def _pick(n, pref):
    while n % pref:
        pref //= 2
    return pref


def kernel(x, p, ln_mix, w_in, qk_gain, lam_params, subln_gain, fgate_bias, rel_bias, w_out,
           ln_ffn, w_up, conv_w, conv_b, w_down, ln_ple, w_ple_gate, w_ple_proj):
    b, s, d = x.shape
    depth = w_in.shape[0]
    t = b * s
    tm_proj = _pick(s, 512)
    tn_proj = 512
    tm_mlp = _pick(s, 512)
    tm_row = _pick(s, 512)
    tq = _pick(s, 512)
    tk = _pick(s, 512)

    bd = _block_diag_ones(256)
    tri = jnp.asarray(np.triu(np.ones((tm_proj, tm_proj), np.float32)), BF16)

    h = x.reshape(t, d).astype(F32)
    for i in range(depth):
        lam_init = 0.8 - 0.6 * math.exp(-0.3 * i)
        w, wfg, gains = _prep_w_in(w_in[i], qk_gain[i])
        fb = jnp.zeros((FG_ROWS, LANES), F32).at[:H_FOX, :].set(
            jnp.broadcast_to(fgate_bias[i].astype(F32)[:, None], (H_FOX, LANES)))
        z, fcum = _proj_in(h, ln_mix[i][None, :].astype(F32), w, wfg, fb, gains, bd, tri,
                           batch=b, seq=s, tm=tm_proj, tn=tn_proj)
        z = z.reshape(b, s, Z_COLS)
        subln2 = jnp.tile(subln_gain[i].astype(F32), HEADS_PER_BLOCK)[None, :]
        oa = _diff_attention(z, lam_params[i].astype(F32), subln2, tq=tq, tk=tk, lam_init=lam_init)
        ob = _fox_attention(z, fcum, tq=tq, tk=tk)
        oc = _chunk_attention(z, _chunk_bias_table(rel_bias[i]))
        h = _out_proj(h, oa.reshape(t, W_DIFF), ob.reshape(t, W_FOX), oc.reshape(t, W_CHUNK),
                      w_out[i].astype(BF16), tm=tm_row)
        h = _mlp(h, ln_ffn[i][None, :].astype(F32), w_up[i].astype(BF16), conv_w[i].astype(F32),
                 conv_b[i][None, :].astype(F32), w_down[i].astype(BF16), seq=s, tm=tm_mlp)
        h = _ple(h, p[i].reshape(t, PLE_DIM), ln_ple[i][None, :].astype(F32),
                 w_ple_gate[i].astype(BF16), w_ple_proj[i].astype(BF16), tm=tm_row)
    return h.reshape(b, s, d).astype(x.dtype)
```

```python
import functools
import math

import numpy as np
import jax
import jax.numpy as jnp
from jax import lax
from jax.experimental import pallas as pl
from jax.experimental.pallas import tpu as pltpu

F32 = jnp.float32
BF16 = jnp.bfloat16

D_MODEL = 1024
CHUNK = 64
HEAD_DIM = 64
H_DIFF = 4
H_FOX = 6
H_CHUNK = 6
N_PREV_CHUNKS = 8
REL_CLIP = 128
D_FF = 2816
CONV_WIDTH = 3
PLE_DIM = 256
EPS = 1e-6
NEG_INF = -1e30
LOG2E = math.log2(math.e)

W_DIFF = H_DIFF * HEAD_DIM
W_FOX = H_FOX * HEAD_DIM
W_CHUNK = H_CHUNK * HEAD_DIM
MIX_WIDTH = W_DIFF + W_FOX + W_CHUNK
ALIBI_SLOPES = tuple(2.0 ** (-8.0 * (h + 1) / H_DIFF) for h in range(H_DIFF))

LANES = 128
HEADS_PER_BLOCK = LANES // HEAD_DIM
FG_ROWS = 8

_Z_SIZES = (W_DIFF,) * 4 + (W_FOX,) * 2 + (W_CHUNK,) * 2 + (W_DIFF, W_FOX, W_CHUNK)
_Z_OFFS = tuple(int(c) for c in np.cumsum((0,) + _Z_SIZES))
Z_COLS = _Z_OFFS[-1]
Z_NORM_COLS = _Z_OFFS[8]
(ZB_Q1, ZB_Q2, ZB_K1, ZB_K2, ZB_QF, ZB_KF, ZB_QC, ZB_KC, ZB_VA, ZB_VF, ZB_VC) = (
    o // LANES for o in _Z_OFFS[:-1])

VMEM_LIMIT = 56 * 1024 * 1024


def _nt_dot(a, b):
    return lax.dot_general(a, b, (((1,), (1,)), ((), ())), preferred_element_type=F32)


def _dot(a, b):
    return jnp.dot(a, b, preferred_element_type=F32)


def _rms_rows(x, g):
    ms = jnp.mean(x * x, axis=-1, keepdims=True)
    return (x * lax.rsqrt(ms + EPS)) * g


def _proj_in_kernel(h_ref, g_ref, w_ref, wfg_ref, fb_ref, gain_ref, bd_ref, tri_ref,
                    z_ref, f_ref, carry_scr, *, tiles_per_batch, tn):
    i = pl.program_id(0)
    tm = h_ref.shape[0]
    hnb = _rms_rows(h_ref[...], g_ref[...]).astype(BF16)

    x = _nt_dot(wfg_ref[...], hnb) + fb_ref[:, 0:1]
    logf = jnp.minimum(x, 0.0) - jnp.log1p(jnp.exp(-jnp.abs(x)))
    hi = logf.astype(BF16)
    r1 = logf - hi.astype(F32)
    mid = r1.astype(BF16)
    lo = (r1 - mid.astype(F32)).astype(BF16)
    tri = tri_ref[...]
    cs = _dot(hi, tri) + _dot(mid, tri) + _dot(lo, tri)

    @pl.when(i % tiles_per_batch == 0)
    def _():
        carry_scr[...] = jnp.zeros_like(carry_scr)

    fcum = cs + carry_scr[:, 0:1]
    f_ref[0] = fcum * LOG2E
    carry_scr[...] = jnp.broadcast_to(fcum[:, tm - 1:tm], carry_scr.shape)

    bd = bd_ref[...]
    w = bd.shape[0]
    n_blocks = Z_COLS // tn
    project = lambda c: _dot(hnb, w_ref[:, c * tn:(c + 1) * tn])
    z_next = project(0)
    for c in range(n_blocks):
        z = z_next
        if c + 1 < n_blocks:
            z_next = project(c + 1)
        if (c + 1) * tn <= Z_NORM_COLS:
            for s in range(tn // w):
                cols = slice(c * tn + s * w, c * tn + (s + 1) * w)
                zz = z[:, s * w:(s + 1) * w]
                ss = _dot((zz * zz).astype(BF16), bd)
                inv = lax.rsqrt(ss * (1.0 / HEAD_DIM) + EPS)
                z_ref[:, cols] = (zz * inv * gain_ref[:, cols]).astype(BF16)
        else:
            z_ref[:, c * tn:(c + 1) * tn] = z.astype(BF16)


def _proj_in(h, g, w, wfg, fb, gain, bd, tri, *, batch, seq, tm, tn):
    t = h.shape[0]
    tiles_per_batch = seq // tm
    assert Z_NORM_COLS % tn == 0 and Z_COLS % tn == 0 and tn % bd.shape[0] == 0
    kern = functools.partial(_proj_in_kernel, tiles_per_batch=tiles_per_batch, tn=tn)
    const = lambda i: (0, 0)
    return pl.pallas_call(
        kern,
        grid=(t // tm,),
        in_specs=[
            pl.BlockSpec((tm, D_MODEL), lambda i: (i, 0)),
            pl.BlockSpec((1, D_MODEL), const),
            pl.BlockSpec((D_MODEL, Z_COLS), const, pipeline_mode=pl.Buffered(1)),
            pl.BlockSpec((FG_ROWS, D_MODEL), const),
            pl.BlockSpec((FG_ROWS, LANES), const),
            pl.BlockSpec((1, Z_COLS), const),
            pl.BlockSpec(bd.shape, const),
            pl.BlockSpec((tm, tm), const),
        ],
        out_specs=[
            pl.BlockSpec((tm, Z_COLS), lambda i: (i, 0)),
            pl.BlockSpec((1, FG_ROWS, tm), lambda i: (i // tiles_per_batch, 0, i % tiles_per_batch)),
        ],
        out_shape=[
            jax.ShapeDtypeStruct((t, Z_COLS), BF16),
            jax.ShapeDtypeStruct((batch, FG_ROWS, seq), F32),
        ],
        scratch_shapes=[pltpu.VMEM((FG_ROWS, LANES), F32)],
        compiler_params=pltpu.CompilerParams(
            dimension_semantics=("arbitrary",), vmem_limit_bytes=VMEM_LIMIT),
        name="proj_in",
    )(h, g, w, wfg, fb, gain, bd, tri)


def _head_lane_mask(h):
    lane = lax.broadcasted_iota(jnp.int32, (1, LANES), 1)
    return (lane >= h * HEAD_DIM) & (lane < (h + 1) * HEAD_DIM)


def _ones_lane_mask(h):
    lane = lax.broadcasted_iota(jnp.int32, (1, LANES), 1)
    return lane == ((h + 1) % HEADS_PER_BLOCK) * HEAD_DIM


def _values_with_ones(v, h):
    fill = jnp.where(_ones_lane_mask(h), 1.0, 0.0).astype(v.dtype)
    return jnp.where(_head_lane_mask(h), v, jnp.broadcast_to(fill, v.shape))


def _online_update(carry, s, vt):
    m, acc = carry
    m_new = jnp.maximum(m, jnp.max(s, axis=-1, keepdims=True))
    alpha = jnp.exp2(m - m_new)
    p = jnp.exp2(s - m_new).astype(BF16)
    return m_new, alpha * acc + _dot(p, vt)


def _init_carry(tq):
    return (jnp.full((tq, 1), NEG_INF, F32), jnp.zeros((tq, LANES), F32))


def _normalised(acc, h):
    l = jnp.sum(jnp.where(_ones_lane_mask(h), acc, 0.0), axis=-1, keepdims=True)
    return jnp.where(_head_lane_mask(h), acc / l, 0.0)


def _fox_kernel(q_ref, k_ref, v_ref, f_ref, o_ref, *, tq, tk):
    hp = pl.program_id(1)
    qi = pl.program_id(2)
    q = q_ref[0]
    row = qi * tq + lax.broadcasted_iota(jnp.int32, (tq, 1), 0)
    n_full = (qi * tq) // tk
    heads = range(HEADS_PER_BLOCK)
    qh = [jnp.where(_head_lane_mask(h), q, jnp.zeros_like(q)) for h in heads]

    def scores(j):
        kt = k_ref[0, pl.ds(pl.multiple_of(j * tk, tk), tk), :]
        return tuple(_nt_dot(qh[h], kt) for h in heads)

    def update(j, ss, carry, diag):
        k0 = pl.multiple_of(j * tk, tk)
        v = v_ref[0, pl.ds(k0, tk), :]
        out = []
        for h in heads:
            s = ss[h] - f_ref[0, pl.ds(hp * HEADS_PER_BLOCK + h, 1), pl.ds(k0, tk)]
            if diag:
                col = k0 + lax.broadcasted_iota(jnp.int32, (1, tk), 1)
                s = jnp.where(col <= row, s, NEG_INF)
            out.append(_online_update(carry[h], s, _values_with_ones(v, h)))
        return tuple(out)

    def body(j, carry):
        return update(j, scores(j), carry, diag=False)

    carry = lax.fori_loop(0, n_full, body, tuple(_init_carry(tq) for _ in heads))
    carry = update(n_full, scores(n_full), carry, diag=True)
    out = _normalised(carry[0][1], 0)
    for h in heads[1:]:
        out = out + _normalised(carry[h][1], h)
    o_ref[0] = out.astype(BF16)


def _fox_attention(z, fcum, *, tq, tk):
    b, s, _ = z.shape
    nblk = W_FOX // LANES
    return pl.pallas_call(
        functools.partial(_fox_kernel, tq=tq, tk=tk),
        grid=(b, nblk, s // tq),
        in_specs=[
            pl.BlockSpec((1, tq, LANES), lambda bi, hp, qi: (bi, qi, ZB_QF + hp)),
            pl.BlockSpec((1, s, LANES), lambda bi, hp, qi: (bi, 0, ZB_KF + hp)),
            pl.BlockSpec((1, s, LANES), lambda bi, hp, qi: (bi, 0, ZB_VF + hp)),
            pl.BlockSpec((1, FG_ROWS, s), lambda bi, hp, qi: (bi, 0, 0)),
        ],
        out_specs=pl.BlockSpec((1, tq, LANES), lambda bi, hp, qi: (bi, qi, hp)),
        out_shape=jax.ShapeDtypeStruct((b, s, W_FOX), BF16),
        compiler_params=pltpu.CompilerParams(
            dimension_semantics=("parallel", "parallel", "arbitrary"), vmem_limit_bytes=VMEM_LIMIT),
        name="fox_attn",
    )(z, z, z, fcum)


def _diff_kernel(q1_ref, q2_ref, k1_ref, k2_ref, v_ref, lp_ref, sg_ref, o_ref, *, tq, tk, lam_init):
    hp = pl.program_id(1)
    qi = pl.program_id(2)
    q1 = q1_ref[0]
    q2 = q2_ref[0]
    row = qi * tq + lax.broadcasted_iota(jnp.int32, (tq, 1), 0)
    n_full = (qi * tq) // tk
    n_all = ((qi + 1) * tq + tk - 1) // tk
    heads = range(HEADS_PER_BLOCK)

    lp = lp_ref[...]
    lam = (jnp.exp(jnp.sum(lp[0:1] * lp[1:2], axis=-1, keepdims=True))
           - jnp.exp(jnp.sum(lp[2:3] * lp[3:4], axis=-1, keepdims=True)) + lam_init)

    q1h = [jnp.where(_head_lane_mask(h), q1, jnp.zeros_like(q1)) for h in heads]
    q2h = [jnp.where(_head_lane_mask(h), q2, jnp.zeros_like(q2)) for h in heads]
    slope = [jnp.where(hp == 0, ALIBI_SLOPES[h] * LOG2E, ALIBI_SLOPES[HEADS_PER_BLOCK + h] * LOG2E)
             .astype(F32) for h in heads]

    def tile(j, carry, diag):
        k0 = pl.multiple_of(j * tk, tk)
        k1t = k1_ref[0, pl.ds(k0, tk), :]
        k2t = k2_ref[0, pl.ds(k0, tk), :]
        v = v_ref[0, pl.ds(k0, tk), :]
        col = k0 + lax.broadcasted_iota(jnp.int32, (1, tk), 1)
        ss1 = [_nt_dot(q1h[h], k1t) for h in heads]
        ss2 = [_nt_dot(q2h[h], k2t) for h in heads]
        out = []
        for h in heads:
            vt = _values_with_ones(v, h)
            s1 = ss1[h]
            s2 = ss2[h]
            if diag:
                bias = slope[h] * jnp.minimum(col, 2 * row - col).astype(F32)
                allowed = (col // CHUNK) <= (row // CHUNK)
                s1 = jnp.where(allowed, s1 + bias, NEG_INF)
                s2 = jnp.where(allowed, s2 + bias, NEG_INF)
            else:
                bias = slope[h] * col.astype(F32)
                s1 = s1 + bias
                s2 = s2 + bias
            c1, c2 = carry[h]
            out.append((_online_update(c1, s1, vt), _online_update(c2, s2, vt)))
        return tuple(out)

    carry = tuple((_init_carry(tq), _init_carry(tq)) for _ in heads)
    carry = lax.fori_loop(0, n_full, functools.partial(tile, diag=False), carry)
    carry = lax.fori_loop(n_full, n_all, functools.partial(tile, diag=True), carry)
    out = jnp.zeros((tq, LANES), F32)
    for h in heads:
        (_, a1), (_, a2) = carry[h]
        out = out + (_normalised(a1, h) - lam * _normalised(a2, h))

    sq = out * out
    inv = jnp.zeros((tq, LANES), F32)
    for h in heads:
        hm = _head_lane_mask(h)
        ms = jnp.sum(jnp.where(hm, sq, 0.0), axis=-1, keepdims=True) * (1.0 / HEAD_DIM)
        inv = jnp.where(hm, lax.rsqrt(ms + EPS), inv)
    o_ref[0] = ((out * inv) * sg_ref[...] * (1.0 - lam_init)).astype(BF16)


def _diff_attention(z, lam_params, subln2, *, tq, tk, lam_init):
    b, s, _ = z.shape
    nblk = W_DIFF // LANES
    return pl.pallas_call(
        functools.partial(_diff_kernel, tq=tq, tk=tk, lam_init=lam_init),
        grid=(b, nblk, s // tq),
        in_specs=[
            pl.BlockSpec((1, tq, LANES), lambda bi, hp, qi: (bi, qi, ZB_Q1 + hp)),
            pl.BlockSpec((1, tq, LANES), lambda bi, hp, qi: (bi, qi, ZB_Q2 + hp)),
            pl.BlockSpec((1, s, LANES), lambda bi, hp, qi: (bi, 0, ZB_K1 + hp)),
            pl.BlockSpec((1, s, LANES), lambda bi, hp, qi: (bi, 0, ZB_K2 + hp)),
            pl.BlockSpec((1, s, LANES), lambda bi, hp, qi: (bi, 0, ZB_VA + hp)),
            pl.BlockSpec((4, HEAD_DIM), lambda bi, hp, qi: (0, 0)),
            pl.BlockSpec((1, LANES), lambda bi, hp, qi: (0, 0)),
        ],
        out_specs=pl.BlockSpec((1, tq, LANES), lambda bi, hp, qi: (bi, qi, hp)),
        out_shape=jax.ShapeDtypeStruct((b, s, W_DIFF), BF16),
        compiler_params=pltpu.CompilerParams(
            dimension_semantics=("parallel", "parallel", "arbitrary"), vmem_limit_bytes=VMEM_LIMIT),
        name="diff_attn",
    )(z, z, z, z, z, lam_params, subln2)


CHUNK_TILE = 256
CHUNK_NTILES = 1 + (N_PREV_CHUNKS * CHUNK) // CHUNK_TILE
CHUNK_QTILES = 4


def _chunk_kernel(q_ref, k_ref, v_ref, bm_ref, o_ref):
    t = CHUNK_TILE
    heads = range(HEADS_PER_BLOCK)
    subs = range(CHUNK_QTILES)
    tiles = range(CHUNK_NTILES)
    qts = [pl.program_id(2) * CHUNK_QTILES + u for u in subs]
    k0s = [[pl.multiple_of(jnp.maximum(qts[u] - d, 0) * t, t) for d in tiles] for u in subs]
    raw = []
    for u in subs:
        q = q_ref[0, u * t:(u + 1) * t, :]
        qh = [jnp.where(_head_lane_mask(h), q, jnp.zeros_like(q)) for h in heads]
        raw.append([[_nt_dot(qh[h], k_ref[0, pl.ds(k0s[u][d], t), :]) for d in tiles] for h in heads])
    for u in subs:
        out = jnp.zeros((t, LANES), F32)
        for h in heads:
            ss = []
            vs = []
            for d in tiles:
                s = raw[u][h][d] + bm_ref[h, d]
                ss.append(jnp.where(qts[u] - d >= 0, s, NEG_INF))
                vs.append(_values_with_ones(v_ref[0, pl.ds(k0s[u][d], t), :], h))
            m = ss[0].max(axis=-1, keepdims=True)
            for s in ss[1:]:
                m = jnp.maximum(m, s.max(axis=-1, keepdims=True))
            acc = jnp.zeros((t, LANES), F32)
            for s, vt in zip(ss, vs):
                acc = acc + _dot(jnp.exp2(s - m).astype(BF16), vt)
            out = out + _normalised(acc, h)
        o_ref[0, u * t:(u + 1) * t, :] = out.astype(BF16)


def _chunk_attention(z, bm):
    b, s, _ = z.shape
    t = CHUNK_TILE
    tq = CHUNK_TILE * CHUNK_QTILES
    nblk = W_CHUNK // LANES
    return pl.pallas_call(
        _chunk_kernel,
        grid=(b, nblk, s // tq),
        in_specs=[
            pl.BlockSpec((1, tq, LANES), lambda bi, hp, qi: (bi, qi, ZB_QC + hp)),
            pl.BlockSpec((1, s, LANES), lambda bi, hp, qi: (bi, 0, ZB_KC + hp)),
            pl.BlockSpec((1, s, LANES), lambda bi, hp, qi: (bi, 0, ZB_VC + hp)),
            pl.BlockSpec((HEADS_PER_BLOCK, CHUNK_NTILES, t, t), lambda bi, hp, qi: (hp, 0, 0, 0)),
        ],
        out_specs=pl.BlockSpec((1, tq, LANES), lambda bi, hp, qi: (bi, qi, hp)),
        out_shape=jax.ShapeDtypeStruct((b, s, W_CHUNK), BF16),
        compiler_params=pltpu.CompilerParams(
            dimension_semantics=("parallel", "parallel", "arbitrary"), vmem_limit_bytes=VMEM_LIMIT),
        name="chunk_attn",
    )(z, z, z, bm)


def _chunk_bias_table(rel_table):
    t = CHUNK_TILE
    period = 2 * t
    u = np.arange(period)
    rel = np.where(u < t, -u, period - u)
    idx = np.stack([np.clip(rel + t * d, -REL_CLIP, REL_CLIP) + REL_CLIP for d in range(CHUNK_NTILES)])
    vec = rel_table.astype(F32)[:, idx] * LOG2E
    flat = jnp.tile(vec, (1, 1, t))[..., :t * (period - 1)]
    toep = flat.reshape(vec.shape[0], CHUNK_NTILES, t, period - 1)[..., :t]
    a = np.arange(t)[:, None]
    b = np.arange(t)[None, :]
    valid = []
    for d in range(CHUNK_NTILES):
        qc = a // CHUNK + (t // CHUNK) * d
        kc = b // CHUNK
        valid.append((kc <= qc) & (kc >= qc - N_PREV_CHUNKS))
    return jnp.where(np.stack(valid)[None], toep, NEG_INF)


MLP_CW = 256
HALO = 8
MLP_LEAD = 3


def _channel_kernel(h_ref, oa_ref, ob_ref, oc_ref, wo_ref, g_ref, wup_ref, cw_ref, cb_ref, wdn_ref,
                    p_ref, gp_ref, wg_ref, wp_ref, o_ref,
                    cat_scr, hn_scr, ubuf, tail, res, acc, *, tiles_per_batch):
    i = pl.program_id(0)
    tm = h_ref.shape[0]

    @pl.when(i % tiles_per_batch == 0)
    def _():
        tail[...] = jnp.zeros_like(tail)

    cat_scr[:, 0:W_DIFF] = oa_ref[...]
    cat_scr[:, W_DIFF:W_DIFF + W_FOX] = ob_ref[...]
    cat_scr[:, W_DIFF + W_FOX:MIX_WIDTH] = oc_ref[...]
    res[...] = h_ref[...] + _dot(cat_scr[...], wo_ref[...])
    hn_scr[...] = _rms_rows(res[...], g_ref[...]).astype(BF16)

    def up(c):
        g0 = c * MLP_CW
        v0 = D_FF + c * MLP_CW
        return (_dot(hn_scr[...], wup_ref[:, g0:g0 + MLP_CW]), _dot(hn_scr[...], wup_ref[:, v0:v0 + MLP_CW]))

    def conv(u, c0, slot):
        cols = slice(c0, c0 + MLP_CW)
        ubuf[slot, 0:HALO, :] = tail[:, cols]
        ubuf[slot, HALO:HALO + tm, :] = u
        tail[:, cols] = u[tm - HALO:tm, :]
        return (cb_ref[:, cols]
                + cw_ref[0:1, cols] * ubuf[slot, HALO - 2:HALO - 2 + tm, :]
                + cw_ref[1:2, cols] * ubuf[slot, HALO - 1:HALO - 1 + tm, :]
                + cw_ref[2:3, cols] * u)

    n_chunks = D_FF // MLP_CW
    ahead = [up(c) for c in range(min(MLP_LEAD, n_chunks))]
    for c in range(n_chunks):
        ug, uv = ahead.pop(0)
        if c + MLP_LEAD < n_chunks:
            ahead.append(up(c + MLP_LEAD))
        gate = conv(ug, c * MLP_CW, 0)
        val = conv(uv, D_FF + c * MLP_CW, 1)
        act = (gate * jax.nn.sigmoid(gate) * val).astype(BF16)
        part = _dot(act, wdn_ref[c * MLP_CW:(c + 1) * MLP_CW, :])
        if c == 0:
            acc[...] = part
        else:
            acc[...] += part

    h2 = res[...] + acc[...]
    hn2 = _rms_rows(h2, gp_ref[...]).astype(BF16)
    gate = jax.nn.sigmoid(_dot(hn2, wg_ref[...]))
    proj = _dot(p_ref[...].astype(BF16), wp_ref[...])
    o_ref[...] = h2 + proj * gate


def _channel_mix(h, oa, ob, oc, wo, g, wup, cw, cb, wdn, p, gp, wg, wp, *, seq, tm):
    t = h.shape[0]
    const = lambda i: (0, 0)
    rows = lambda width: pl.BlockSpec((tm, width), lambda i: (i, 0))
    resident = lambda shape: pl.BlockSpec(shape, const, pipeline_mode=pl.Buffered(1))
    return pl.pallas_call(
        functools.partial(_channel_kernel, tiles_per_batch=seq // tm),
        grid=(t // tm,),
        in_specs=[
            rows(D_MODEL), rows(W_DIFF), rows(W_FOX), rows(W_CHUNK),
            resident((MIX_WIDTH, D_MODEL)),
            pl.BlockSpec((1, D_MODEL), const),
            resident((D_MODEL, 2 * D_FF)),
            pl.BlockSpec((CONV_WIDTH, 2 * D_FF), const),
            pl.BlockSpec((1, 2 * D_FF), const),
            resident((D_FF, D_MODEL)),
            rows(PLE_DIM),
            pl.BlockSpec((1, D_MODEL), const),
            resident((D_MODEL, D_MODEL)),
            resident((PLE_DIM, D_MODEL)),
        ],
        out_specs=rows(D_MODEL),
        out_shape=jax.ShapeDtypeStruct((t, D_MODEL), F32),
        scratch_shapes=[
            pltpu.VMEM((tm, MIX_WIDTH), BF16),
            pltpu.VMEM((tm, D_MODEL), BF16),
            pltpu.VMEM((2, HALO + tm, MLP_CW), F32),
            pltpu.VMEM((HALO, 2 * D_FF), F32),
            pltpu.VMEM((tm, D_MODEL), F32),
            pltpu.VMEM((tm, D_MODEL), F32),
        ],
        compiler_params=pltpu.CompilerParams(
            dimension_semantics=("arbitrary",), vmem_limit_bytes=VMEM_LIMIT),
        name="channel_mix",
    )(h, oa, ob, oc, wo, g, wup, cw, cb, wdn, p, gp, wg, wp)


def _prep_w_in(w_in, qk_gain):
    sizes = (W_DIFF,) * 5 + (W_FOX,) * 3 + (H_FOX,) + (W_CHUNK,) * 3
    offs = np.cumsum((0,) + sizes)
    seg = lambda n: w_in[:, offs[n]:offs[n + 1]]
    q1, q2, k1, k2, va, qf, kf, vf, fg, qc, kc, vc = (seg(n) for n in range(12))
    w = jnp.concatenate([q1, q2, k1, k2, qf, kf, qc, kc, va, vf, vc], axis=1).astype(BF16)
    wfg = jnp.zeros((FG_ROWS, D_MODEL), F32).at[:H_FOX].set(fg.T).astype(BF16)
    scale = HEAD_DIM ** -0.5 * LOG2E
    g = qk_gain.astype(F32)
    gains = jnp.concatenate([
        jnp.tile(g[0] * scale, H_DIFF), jnp.tile(g[0] * scale, H_DIFF),
        jnp.tile(g[1], H_DIFF), jnp.tile(g[1], H_DIFF),
        jnp.tile(g[2] * scale, H_FOX), jnp.tile(g[3], H_FOX),
        jnp.tile(g[4] * scale, H_CHUNK), jnp.tile(g[5], H_CHUNK),
        jnp.ones((Z_COLS - Z_NORM_COLS,), F32)])[None, :]
    return w, wfg, gains


def _block_diag_ones(width):
    r = np.arange(width)
    return jnp.asarray((r[:, None] // HEAD_DIM) == (r[None, :] // HEAD_DIM), BF16)


def _pick(n, pref):
    while n % pref:
        pref //= 2
    return pref


def kernel(x, p, ln_mix, w_in, qk_gain, lam_params, subln_gain, fgate_bias, rel_bias, w_out,
           ln_ffn, w_up, conv_w, conv_b, w_down, ln_ple, w_ple_gate, w_ple_proj):
    b, s, d = x.shape
    depth = w_in.shape[0]
    t = b * s
    tm_proj = _pick(s, 512)
    tn_proj = 512
    tm_mlp = _pick(s, 512)
    tq = _pick(s, 512)
    tk = _pick(s, 512)

    bd = _block_diag_ones(256)
    tri = jnp.asarray(np.triu(np.ones((tm_proj, tm_proj), np.float32)), BF16)

    h = x.reshape(t, d).astype(F32)
    for i in range(depth):
        lam_init = 0.8 - 0.6 * math.exp(-0.3 * i)
        w, wfg, gains = _prep_w_in(w_in[i], qk_gain[i])
        fb = jnp.zeros((FG_ROWS, LANES), F32).at[:H_FOX, :].set(
            jnp.broadcast_to(fgate_bias[i].astype(F32)[:, None], (H_FOX, LANES)))
        z, fcum = _proj_in(h, ln_mix[i][None, :].astype(F32), w, wfg, fb, gains, bd, tri,
                           batch=b, seq=s, tm=tm_proj, tn=tn_proj)
        z = z.reshape(b, s, Z_COLS)
        subln2 = jnp.tile(subln_gain[i].astype(F32), HEADS_PER_BLOCK)[None, :]
        oa = _diff_attention(z, lam_params[i].astype(F32), subln2, tq=tq, tk=tk, lam_init=lam_init)
        ob = _fox_attention(z, fcum, tq=tq, tk=tk)
        oc = _chunk_attention(z, _chunk_bias_table(rel_bias[i]))
        h = _channel_mix(
            h, oa.reshape(t, W_DIFF), ob.reshape(t, W_FOX), oc.reshape(t, W_CHUNK), w_out[i].astype(BF16),
            ln_ffn[i][None, :].astype(F32), w_up[i].astype(BF16), conv_w[i].astype(F32),
            conv_b[i][None, :].astype(F32), w_down[i].astype(BF16),
            p[i].reshape(t, PLE_DIM), ln_ple[i][None, :].astype(F32),
            w_ple_gate[i].astype(BF16), w_ple_proj[i].astype(BF16), seq=s, tm=tm_mlp)
    return h.reshape(b, s, d).astype(x.dtype)
```

```python
import functools
import math

import numpy as np
import jax
import jax.numpy as jnp
from jax import lax
from jax.experimental import pallas as pl
from jax.experimental.pallas import tpu as pltpu

F32 = jnp.float32
BF16 = jnp.bfloat16

D_MODEL = 1024
CHUNK = 64
HEAD_DIM = 64
H_DIFF = 4
H_FOX = 6
H_CHUNK = 6
N_PREV_CHUNKS = 8
REL_CLIP = 128
D_FF = 2816
CONV_WIDTH = 3
PLE_DIM = 256
EPS = 1e-6
NEG_INF = -1e30
LOG2E = math.log2(math.e)

W_DIFF = H_DIFF * HEAD_DIM
W_FOX = H_FOX * HEAD_DIM
W_CHUNK = H_CHUNK * HEAD_DIM
MIX_WIDTH = W_DIFF + W_FOX + W_CHUNK
ALIBI_SLOPES = tuple(2.0 ** (-8.0 * (h + 1) / H_DIFF) for h in range(H_DIFF))

LANES = 128
HEADS_PER_BLOCK = LANES // HEAD_DIM
FG_ROWS = 8

_Z_SIZES = (W_DIFF,) * 4 + (W_FOX,) * 2 + (W_CHUNK,) * 2 + (W_DIFF, W_FOX, W_CHUNK)
_Z_OFFS = tuple(int(c) for c in np.cumsum((0,) + _Z_SIZES))
Z_COLS = _Z_OFFS[-1]
Z_NORM_COLS = _Z_OFFS[8]
(ZB_Q1, ZB_Q2, ZB_K1, ZB_K2, ZB_QF, ZB_KF, ZB_QC, ZB_KC, ZB_VA, ZB_VF, ZB_VC) = (
    o // LANES for o in _Z_OFFS[:-1])

VMEM_LIMIT = 56 * 1024 * 1024


def _nt_dot(a, b):
    return lax.dot_general(a, b, (((1,), (1,)), ((), ())), preferred_element_type=F32)


def _dot(a, b):
    return jnp.dot(a, b, preferred_element_type=F32)


def _rms_rows(x, g):
    ms = jnp.mean(x * x, axis=-1, keepdims=True)
    return (x * lax.rsqrt(ms + EPS)) * g


def _proj_in_kernel(h_ref, g_ref, w_ref, wfg_ref, fb_ref, gain_ref, bd_ref, tri_ref,
                    z_ref, f_ref, carry_scr, *, tiles_per_batch, tn):
    i = pl.program_id(0)
    tm = h_ref.shape[0]
    hnb = _rms_rows(h_ref[...], g_ref[...]).astype(BF16)

    x = _nt_dot(wfg_ref[...], hnb) + fb_ref[:, 0:1]
    logf = jnp.minimum(x, 0.0) - jnp.log1p(jnp.exp(-jnp.abs(x)))
    hi = logf.astype(BF16)
    r1 = logf - hi.astype(F32)
    mid = r1.astype(BF16)
    lo = (r1 - mid.astype(F32)).astype(BF16)
    tri = tri_ref[...]
    cs = _dot(hi, tri) + _dot(mid, tri) + _dot(lo, tri)

    @pl.when(i % tiles_per_batch == 0)
    def _():
        carry_scr[...] = jnp.zeros_like(carry_scr)

    fcum = cs + carry_scr[:, 0:1]
    f_ref[0] = fcum * LOG2E
    carry_scr[...] = jnp.broadcast_to(fcum[:, tm - 1:tm], carry_scr.shape)

    bd = bd_ref[...]
    w = bd.shape[0]
    n_blocks = Z_COLS // tn
    project = lambda c: _dot(hnb, w_ref[:, c * tn:(c + 1) * tn])
    z_next = project(0)
    for c in range(n_blocks):
        z = z_next
        if c + 1 < n_blocks:
            z_next = project(c + 1)
        if (c + 1) * tn <= Z_NORM_COLS:
            for s in range(tn // w):
                cols = slice(c * tn + s * w, c * tn + (s + 1) * w)
                zz = z[:, s * w:(s + 1) * w]
                ss = _dot((zz * zz).astype(BF16), bd)
                inv = lax.rsqrt(ss * (1.0 / HEAD_DIM) + EPS)
                z_ref[:, cols] = (zz * inv * gain_ref[:, cols]).astype(BF16)
        else:
            z_ref[:, c * tn:(c + 1) * tn] = z.astype(BF16)


def _proj_in(h, g, w, wfg, fb, gain, bd, tri, *, batch, seq, tm, tn):
    t = h.shape[0]
    tiles_per_batch = seq // tm
    assert Z_NORM_COLS % tn == 0 and Z_COLS % tn == 0 and tn % bd.shape[0] == 0
    kern = functools.partial(_proj_in_kernel, tiles_per_batch=tiles_per_batch, tn=tn)
    const = lambda i: (0, 0)
    return pl.pallas_call(
        kern,
        grid=(t // tm,),
        in_specs=[
            pl.BlockSpec((tm, D_MODEL), lambda i: (i, 0)),
            pl.BlockSpec((1, D_MODEL), const),
            pl.BlockSpec((D_MODEL, Z_COLS), const, pipeline_mode=pl.Buffered(1)),
            pl.BlockSpec((FG_ROWS, D_MODEL), const),
            pl.BlockSpec((FG_ROWS, LANES), const),
            pl.BlockSpec((1, Z_COLS), const),
            pl.BlockSpec(bd.shape, const),
            pl.BlockSpec((tm, tm), const),
        ],
        out_specs=[
            pl.BlockSpec((tm, Z_COLS), lambda i: (i, 0)),
            pl.BlockSpec((1, FG_ROWS, tm), lambda i: (i // tiles_per_batch, 0, i % tiles_per_batch)),
        ],
        out_shape=[
            jax.ShapeDtypeStruct((t, Z_COLS), BF16),
            jax.ShapeDtypeStruct((batch, FG_ROWS, seq), F32),
        ],
        scratch_shapes=[pltpu.VMEM((FG_ROWS, LANES), F32)],
        compiler_params=pltpu.CompilerParams(
            dimension_semantics=("arbitrary",), vmem_limit_bytes=VMEM_LIMIT),
        name="proj_in",
    )(h, g, w, wfg, fb, gain, bd, tri)


def _head_lane_mask(h):
    lane = lax.broadcasted_iota(jnp.int32, (1, LANES), 1)
    return (lane >= h * HEAD_DIM) & (lane < (h + 1) * HEAD_DIM)


def _ones_lane_mask(h):
    lane = lax.broadcasted_iota(jnp.int32, (1, LANES), 1)
    return lane == ((h + 1) % HEADS_PER_BLOCK) * HEAD_DIM


def _values_with_ones(v, h):
    fill = jnp.where(_ones_lane_mask(h), 1.0, 0.0).astype(v.dtype)
    return jnp.where(_head_lane_mask(h), v, jnp.broadcast_to(fill, v.shape))


def _online_update(carry, s, vt):
    m, acc = carry
    m_new = jnp.maximum(m, jnp.max(s, axis=-1, keepdims=True))
    alpha = jnp.exp2(m - m_new)
    p = jnp.exp2(s - m_new).astype(BF16)
    return m_new, alpha * acc + _dot(p, vt)


def _init_carry(tq):
    return (jnp.full((tq, 1), NEG_INF, F32), jnp.zeros((tq, LANES), F32))


def _normalised(acc, h):
    l = jnp.sum(jnp.where(_ones_lane_mask(h), acc, 0.0), axis=-1, keepdims=True)
    return jnp.where(_head_lane_mask(h), acc / l, 0.0)


def _fox_kernel(q_ref, k_ref, v_ref, f_ref, o_ref, *, tq, tk):
    hp = pl.program_id(1)
    qi = pl.program_id(2)
    q = q_ref[0]
    row = qi * tq + lax.broadcasted_iota(jnp.int32, (tq, 1), 0)
    n_full = (qi * tq) // tk
    heads = range(HEADS_PER_BLOCK)
    qh = [jnp.where(_head_lane_mask(h), q, jnp.zeros_like(q)) for h in heads]

    def update(j, width, carry, diag=False):
        k0 = pl.multiple_of(j * width, width)
        kt = k_ref[0, pl.ds(k0, width), :]
        ss = [_nt_dot(qh[h], kt) for h in heads]
        v = v_ref[0, pl.ds(k0, width), :]
        out = []
        for h in heads:
            s = ss[h] - f_ref[0, pl.ds(hp * HEADS_PER_BLOCK + h, 1), pl.ds(k0, width)]
            if diag:
                col = k0 + lax.broadcasted_iota(jnp.int32, (1, width), 1)
                s = jnp.where(col <= row, s, NEG_INF)
            out.append(_online_update(carry[h], s, _values_with_ones(v, h)))
        return tuple(out)

    n_wide = n_full // 2
    carry = tuple(_init_carry(tq) for _ in heads)
    carry = lax.fori_loop(0, n_wide, lambda j, c: update(j, 2 * tk, c), carry)
    carry = lax.fori_loop(2 * n_wide, n_full, lambda j, c: update(j, tk, c), carry)
    carry = update(n_full, tk, carry, diag=True)
    out = _normalised(carry[0][1], 0)
    for h in heads[1:]:
        out = out + _normalised(carry[h][1], h)
    o_ref[0] = out.astype(BF16)


def _fox_attention(z, fcum, *, tq, tk):
    b, s, _ = z.shape
    nblk = W_FOX // LANES
    return pl.pallas_call(
        functools.partial(_fox_kernel, tq=tq, tk=tk),
        grid=(b, nblk, s // tq),
        in_specs=[
            pl.BlockSpec((1, tq, LANES), lambda bi, hp, qi: (bi, qi, ZB_QF + hp)),
            pl.BlockSpec((1, s, LANES), lambda bi, hp, qi: (bi, 0, ZB_KF + hp)),
            pl.BlockSpec((1, s, LANES), lambda bi, hp, qi: (bi, 0, ZB_VF + hp)),
            pl.BlockSpec((1, FG_ROWS, s), lambda bi, hp, qi: (bi, 0, 0)),
        ],
        out_specs=pl.BlockSpec((1, tq, LANES), lambda bi, hp, qi: (bi, qi, hp)),
        out_shape=jax.ShapeDtypeStruct((b, s, W_FOX), BF16),
        compiler_params=pltpu.CompilerParams(
            dimension_semantics=("parallel", "parallel", "arbitrary"), vmem_limit_bytes=VMEM_LIMIT),
        name="fox_attn",
    )(z, z, z, fcum)


def _diff_kernel(q1_ref, q2_ref, k1_ref, k2_ref, v_ref, lp_ref, sg_ref, o_ref, *, tq, tk, lam_init):
    hp = pl.program_id(1)
    qi = pl.program_id(2)
    q1 = q1_ref[0]
    q2 = q2_ref[0]
    row = qi * tq + lax.broadcasted_iota(jnp.int32, (tq, 1), 0)
    n_full = (qi * tq) // tk
    heads = range(HEADS_PER_BLOCK)

    lp = lp_ref[...]
    lam = (jnp.exp(jnp.sum(lp[0:1] * lp[1:2], axis=-1, keepdims=True))
           - jnp.exp(jnp.sum(lp[2:3] * lp[3:4], axis=-1, keepdims=True)) + lam_init)

    q1h = [jnp.where(_head_lane_mask(h), q1, jnp.zeros_like(q1)) for h in heads]
    q2h = [jnp.where(_head_lane_mask(h), q2, jnp.zeros_like(q2)) for h in heads]
    slope = [jnp.where(hp == 0, ALIBI_SLOPES[h] * LOG2E, ALIBI_SLOPES[HEADS_PER_BLOCK + h] * LOG2E)
             .astype(F32) for h in heads]

    def tile(j, width, carry, diag=False):
        k0 = pl.multiple_of(j * width, width)
        k1t = k1_ref[0, pl.ds(k0, width), :]
        k2t = k2_ref[0, pl.ds(k0, width), :]
        v = v_ref[0, pl.ds(k0, width), :]
        col = k0 + lax.broadcasted_iota(jnp.int32, (1, width), 1)
        ss1 = [_nt_dot(q1h[h], k1t) for h in heads]
        ss2 = [_nt_dot(q2h[h], k2t) for h in heads]
        out = []
        for h in heads:
            vt = _values_with_ones(v, h)
            s1 = ss1[h]
            s2 = ss2[h]
            if diag:
                bias = slope[h] * jnp.minimum(col, 2 * row - col).astype(F32)
                allowed = (col // CHUNK) <= (row // CHUNK)
                s1 = jnp.where(allowed, s1 + bias, NEG_INF)
                s2 = jnp.where(allowed, s2 + bias, NEG_INF)
            else:
                bias = slope[h] * col.astype(F32)
                s1 = s1 + bias
                s2 = s2 + bias
            c1, c2 = carry[h]
            out.append((_online_update(c1, s1, vt), _online_update(c2, s2, vt)))
        return tuple(out)

    n_wide = n_full // 2
    carry = tuple((_init_carry(tq), _init_carry(tq)) for _ in heads)
    carry = lax.fori_loop(0, n_wide, lambda j, c: tile(j, 2 * tk, c), carry)
    carry = lax.fori_loop(2 * n_wide, n_full, lambda j, c: tile(j, tk, c), carry)
    carry = tile(n_full, tk, carry, diag=True)
    out = jnp.zeros((tq, LANES), F32)
    for h in heads:
        (_, a1), (_, a2) = carry[h]
        out = out + (_normalised(a1, h) - lam * _normalised(a2, h))

    sq = out * out
    inv = jnp.zeros((tq, LANES), F32)
    for h in heads:
        hm = _head_lane_mask(h)
        ms = jnp.sum(jnp.where(hm, sq, 0.0), axis=-1, keepdims=True) * (1.0 / HEAD_DIM)
        inv = jnp.where(hm, lax.rsqrt(ms + EPS), inv)
    o_ref[0] = ((out * inv) * sg_ref[...] * (1.0 - lam_init)).astype(BF16)


def _diff_attention(z, lam_params, subln2, *, tq, tk, lam_init):
    b, s, _ = z.shape
    nblk = W_DIFF // LANES
    return pl.pallas_call(
        functools.partial(_diff_kernel, tq=tq, tk=tk, lam_init=lam_init),
        grid=(b, nblk, s // tq),
        in_specs=[
            pl.BlockSpec((1, tq, LANES), lambda bi, hp, qi: (bi, qi, ZB_Q1 + hp)),
            pl.BlockSpec((1, tq, LANES), lambda bi, hp, qi: (bi, qi, ZB_Q2 + hp)),
            pl.BlockSpec((1, s, LANES), lambda bi, hp, qi: (bi, 0, ZB_K1 + hp)),
            pl.BlockSpec((1, s, LANES), lambda bi, hp, qi: (bi, 0, ZB_K2 + hp)),
            pl.BlockSpec((1, s, LANES), lambda bi, hp, qi: (bi, 0, ZB_VA + hp)),
            pl.BlockSpec((4, HEAD_DIM), lambda bi, hp, qi: (0, 0)),
            pl.BlockSpec((1, LANES), lambda bi, hp, qi: (0, 0)),
        ],
        out_specs=pl.BlockSpec((1, tq, LANES), lambda bi, hp, qi: (bi, qi, hp)),
        out_shape=jax.ShapeDtypeStruct((b, s, W_DIFF), BF16),
        compiler_params=pltpu.CompilerParams(
            dimension_semantics=("parallel", "parallel", "arbitrary"), vmem_limit_bytes=VMEM_LIMIT),
        name="diff_attn",
    )(z, z, z, z, z, lam_params, subln2)


CHUNK_TILE = 256
CHUNK_NTILES = 1 + (N_PREV_CHUNKS * CHUNK) // CHUNK_TILE
CHUNK_QTILES = 4


def _chunk_kernel(q_ref, k_ref, v_ref, bm_ref, o_ref):
    t = CHUNK_TILE
    heads = range(HEADS_PER_BLOCK)
    subs = range(CHUNK_QTILES)
    tiles = range(CHUNK_NTILES)
    qts = [pl.program_id(2) * CHUNK_QTILES + u for u in subs]
    k0s = [[pl.multiple_of(jnp.maximum(qts[u] - d, 0) * t, t) for d in tiles] for u in subs]
    raw = []
    for u in subs:
        q = q_ref[0, u * t:(u + 1) * t, :]
        qh = [jnp.where(_head_lane_mask(h), q, jnp.zeros_like(q)) for h in heads]
        raw.append([[_nt_dot(qh[h], k_ref[0, pl.ds(k0s[u][d], t), :]) for d in tiles] for h in heads])
    for u in subs:
        out = jnp.zeros((t, LANES), F32)
        for h in heads:
            ss = []
            vs = []
            for d in tiles:
                s = raw[u][h][d] + bm_ref[h, d]
                ss.append(jnp.where(qts[u] - d >= 0, s, NEG_INF))
                vs.append(_values_with_ones(v_ref[0, pl.ds(k0s[u][d], t), :], h))
            m = ss[0].max(axis=-1, keepdims=True)
            for s in ss[1:]:
                m = jnp.maximum(m, s.max(axis=-1, keepdims=True))
            acc = jnp.zeros((t, LANES), F32)
            for s, vt in zip(ss, vs):
                acc = acc + _dot(jnp.exp2(s - m).astype(BF16), vt)
            out = out + _normalised(acc, h)
        o_ref[0, u * t:(u + 1) * t, :] = out.astype(BF16)


def _chunk_attention(z, bm):
    b, s, _ = z.shape
    t = CHUNK_TILE
    tq = CHUNK_TILE * CHUNK_QTILES
    nblk = W_CHUNK // LANES
    return pl.pallas_call(
        _chunk_kernel,
        grid=(b, nblk, s // tq),
        in_specs=[
            pl.BlockSpec((1, tq, LANES), lambda bi, hp, qi: (bi, qi, ZB_QC + hp)),
            pl.BlockSpec((1, s, LANES), lambda bi, hp, qi: (bi, 0, ZB_KC + hp)),
            pl.BlockSpec((1, s, LANES), lambda bi, hp, qi: (bi, 0, ZB_VC + hp)),
            pl.BlockSpec((HEADS_PER_BLOCK, CHUNK_NTILES, t, t), lambda bi, hp, qi: (hp, 0, 0, 0)),
        ],
        out_specs=pl.BlockSpec((1, tq, LANES), lambda bi, hp, qi: (bi, qi, hp)),
        out_shape=jax.ShapeDtypeStruct((b, s, W_CHUNK), BF16),
        compiler_params=pltpu.CompilerParams(
            dimension_semantics=("parallel", "parallel", "arbitrary"), vmem_limit_bytes=VMEM_LIMIT),
        name="chunk_attn",
    )(z, z, z, bm)


def _chunk_bias_table(rel_table):
    t = CHUNK_TILE
    period = 2 * t
    u = np.arange(period)
    rel = np.where(u < t, -u, period - u)
    idx = np.stack([np.clip(rel + t * d, -REL_CLIP, REL_CLIP) + REL_CLIP for d in range(CHUNK_NTILES)])
    vec = rel_table.astype(F32)[:, idx] * LOG2E
    flat = jnp.tile(vec, (1, 1, t))[..., :t * (period - 1)]
    toep = flat.reshape(vec.shape[0], CHUNK_NTILES, t, period - 1)[..., :t]
    a = np.arange(t)[:, None]
    b = np.arange(t)[None, :]
    valid = []
    for d in range(CHUNK_NTILES):
        qc = a // CHUNK + (t // CHUNK) * d
        kc = b // CHUNK
        valid.append((kc <= qc) & (kc >= qc - N_PREV_CHUNKS))
    return jnp.where(np.stack(valid)[None], toep, NEG_INF)


MLP_CW = 256
HALO = 8
MLP_LEAD = 3


def _channel_kernel(h_ref, oa_ref, ob_ref, oc_ref, wo_ref, g_ref, wup_ref, cw_ref, cb_ref, wdn_ref,
                    p_ref, gp_ref, wg_ref, wp_ref, o_ref,
                    cat_scr, hn_scr, ubuf, tail, res, acc, *, tiles_per_batch):
    i = pl.program_id(0)
    tm = h_ref.shape[0]

    @pl.when(i % tiles_per_batch == 0)
    def _():
        tail[...] = jnp.zeros_like(tail)

    cat_scr[:, 0:W_DIFF] = oa_ref[...]
    cat_scr[:, W_DIFF:W_DIFF + W_FOX] = ob_ref[...]
    cat_scr[:, W_DIFF + W_FOX:MIX_WIDTH] = oc_ref[...]
    res[...] = h_ref[...] + _dot(cat_scr[...], wo_ref[...])
    hn_scr[...] = _rms_rows(res[...], g_ref[...]).astype(BF16)

    def up(c):
        g0 = c * MLP_CW
        v0 = D_FF + c * MLP_CW
        return (_dot(hn_scr[...], wup_ref[:, g0:g0 + MLP_CW]), _dot(hn_scr[...], wup_ref[:, v0:v0 + MLP_CW]))

    def conv(u, c0, slot):
        cols = slice(c0, c0 + MLP_CW)
        ubuf[slot, 0:HALO, :] = tail[:, cols]
        ubuf[slot, HALO:HALO + tm, :] = u
        tail[:, cols] = u[tm - HALO:tm, :]
        return (cb_ref[:, cols]
                + cw_ref[0:1, cols] * ubuf[slot, HALO - 2:HALO - 2 + tm, :]
                + cw_ref[1:2, cols] * ubuf[slot, HALO - 1:HALO - 1 + tm, :]
                + cw_ref[2:3, cols] * u)

    n_chunks = D_FF // MLP_CW
    ahead = [up(c) for c in range(min(MLP_LEAD, n_chunks))]
    for c in range(n_chunks):
        ug, uv = ahead.pop(0)
        if c + MLP_LEAD < n_chunks:
            ahead.append(up(c + MLP_LEAD))
        gate = conv(ug, c * MLP_CW, 0)
        val = conv(uv, D_FF + c * MLP_CW, 1)
        act = (gate * jax.nn.sigmoid(gate) * val).astype(BF16)
        part = _dot(act, wdn_ref[c * MLP_CW:(c + 1) * MLP_CW, :])
        if c == 0:
            acc[...] = part
        else:
            acc[...] += part

    h2 = res[...] + acc[...]
    hn2 = _rms_rows(h2, gp_ref[...]).astype(BF16)
    gate = jax.nn.sigmoid(_dot(hn2, wg_ref[...]))
    proj = _dot(p_ref[...].astype(BF16), wp_ref[...])
    o_ref[...] = h2 + proj * gate


def _channel_mix(h, oa, ob, oc, wo, g, wup, cw, cb, wdn, p, gp, wg, wp, *, seq, tm):
    t = h.shape[0]
    const = lambda i: (0, 0)
    rows = lambda width: pl.BlockSpec((tm, width), lambda i: (i, 0))
    resident = lambda shape: pl.BlockSpec(shape, const, pipeline_mode=pl.Buffered(1))
    return pl.pallas_call(
        functools.partial(_channel_kernel, tiles_per_batch=seq // tm),
        grid=(t // tm,),
        in_specs=[
            rows(D_MODEL), rows(W_DIFF), rows(W_FOX), rows(W_CHUNK),
            resident((MIX_WIDTH, D_MODEL)),
            pl.BlockSpec((1, D_MODEL), const),
            resident((D_MODEL, 2 * D_FF)),
            pl.BlockSpec((CONV_WIDTH, 2 * D_FF), const),
            pl.BlockSpec((1, 2 * D_FF), const),
            resident((D_FF, D_MODEL)),
            rows(PLE_DIM),
            pl.BlockSpec((1, D_MODEL), const),
            resident((D_MODEL, D_MODEL)),
            resident((PLE_DIM, D_MODEL)),
        ],
        out_specs=rows(D_MODEL),
        out_shape=jax.ShapeDtypeStruct((t, D_MODEL), F32),
        scratch_shapes=[
            pltpu.VMEM((tm, MIX_WIDTH), BF16),
            pltpu.VMEM((tm, D_MODEL), BF16),
            pltpu.VMEM((2, HALO + tm, MLP_CW), F32),
            pltpu.VMEM((HALO, 2 * D_FF), F32),
            pltpu.VMEM((tm, D_MODEL), F32),
            pltpu.VMEM((tm, D_MODEL), F32),
        ],
        compiler_params=pltpu.CompilerParams(
            dimension_semantics=("arbitrary",), vmem_limit_bytes=VMEM_LIMIT),
        name="channel_mix",
    )(h, oa, ob, oc, wo, g, wup, cw, cb, wdn, p, gp, wg, wp)


def _prep_w_in(w_in, qk_gain):
    sizes = (W_DIFF,) * 5 + (W_FOX,) * 3 + (H_FOX,) + (W_CHUNK,) * 3
    offs = np.cumsum((0,) + sizes)
    seg = lambda n: w_in[:, offs[n]:offs[n + 1]]
    q1, q2, k1, k2, va, qf, kf, vf, fg, qc, kc, vc = (seg(n) for n in range(12))
    w = jnp.concatenate([q1, q2, k1, k2, qf, kf, qc, kc, va, vf, vc], axis=1).astype(BF16)
    wfg = jnp.zeros((FG_ROWS, D_MODEL), F32).at[:H_FOX].set(fg.T).astype(BF16)
    scale = HEAD_DIM ** -0.5 * LOG2E
    g = qk_gain.astype(F32)
    gains = jnp.concatenate([
        jnp.tile(g[0] * scale, H_DIFF), jnp.tile(g[0] * scale, H_DIFF),
        jnp.tile(g[1], H_DIFF), jnp.tile(g[1], H_DIFF),
        jnp.tile(g[2] * scale, H_FOX), jnp.tile(g[3], H_FOX),
        jnp.tile(g[4] * scale, H_CHUNK), jnp.tile(g[5], H_CHUNK),
        jnp.ones((Z_COLS - Z_NORM_COLS,), F32)])[None, :]
    return w, wfg, gains


def _block_diag_ones(width):
    r = np.arange(width)
    return jnp.asarray((r[:, None] // HEAD_DIM) == (r[None, :] // HEAD_DIM), BF16)


def _pick(n, pref):
    while n % pref:
        pref //= 2
    return pref


def kernel(x, p, ln_mix, w_in, qk_gain, lam_params, subln_gain, fgate_bias, rel_bias, w_out,
           ln_ffn, w_up, conv_w, conv_b, w_down, ln_ple, w_ple_gate, w_ple_proj):
    b, s, d = x.shape
    depth = w_in.shape[0]
    t = b * s
    tm_proj = _pick(s, 512)
    tn_proj = 512
    tm_mlp = _pick(s, 512)
    tq = _pick(s, 512)
    tk = tq

    bd = _block_diag_ones(256)
    tri = jnp.asarray(np.triu(np.ones((tm_proj, tm_proj), np.float32)), BF16)

    h = x.reshape(t, d).astype(F32)
    for i in range(depth):
        lam_init = 0.8 - 0.6 * math.exp(-0.3 * i)
        w, wfg, gains = _prep_w_in(w_in[i], qk_gain[i])
        fb = jnp.zeros((FG_ROWS, LANES), F32).at[:H_FOX, :].set(
            jnp.broadcast_to(fgate_bias[i].astype(F32)[:, None], (H_FOX, LANES)))
        z, fcum = _proj_in(h, ln_mix[i][None, :].astype(F32), w, wfg, fb, gains, bd, tri,
                           batch=b, seq=s, tm=tm_proj, tn=tn_proj)
        z = z.reshape(b, s, Z_COLS)
        subln2 = jnp.tile(subln_gain[i].astype(F32), HEADS_PER_BLOCK)[None, :]
        oa = _diff_attention(z, lam_params[i].astype(F32), subln2, tq=tq, tk=tk, lam_init=lam_init)
        ob = _fox_attention(z, fcum, tq=tq, tk=tk)
        oc = _chunk_attention(z, _chunk_bias_table(rel_bias[i]))
        h = _channel_mix(
            h, oa.reshape(t, W_DIFF), ob.reshape(t, W_FOX), oc.reshape(t, W_CHUNK), w_out[i].astype(BF16),
            ln_ffn[i][None, :].astype(F32), w_up[i].astype(BF16), conv_w[i].astype(F32),
            conv_b[i][None, :].astype(F32), w_down[i].astype(BF16),
            p[i].reshape(t, PLE_DIM), ln_ple[i][None, :].astype(F32),
            w_ple_gate[i].astype(BF16), w_ple_proj[i].astype(BF16), seq=s, tm=tm_mlp)
    return h.reshape(b, s, d).astype(x.dtype)
```

```python
import functools
import math

import numpy as np
import jax
import jax.numpy as jnp
from jax import lax
from jax.experimental import pallas as pl
from jax.experimental.pallas import tpu as pltpu

F32 = jnp.float32
BF16 = jnp.bfloat16

D_MODEL = 1024
CHUNK = 64
HEAD_DIM = 64
H_DIFF = 4
H_FOX = 6
H_CHUNK = 6
N_PREV_CHUNKS = 8
REL_CLIP = 128
D_FF = 2816
CONV_WIDTH = 3
PLE_DIM = 256
EPS = 1e-6
NEG_INF = -1e30
LOG2E = math.log2(math.e)

W_DIFF = H_DIFF * HEAD_DIM
W_FOX = H_FOX * HEAD_DIM
W_CHUNK = H_CHUNK * HEAD_DIM
MIX_WIDTH = W_DIFF + W_FOX + W_CHUNK
ALIBI_SLOPES = tuple(2.0 ** (-8.0 * (h + 1) / H_DIFF) for h in range(H_DIFF))

LANES = 128
HEADS_PER_BLOCK = LANES // HEAD_DIM
FG_ROWS = 8

_Z_SIZES = (W_DIFF,) * 4 + (W_FOX,) * 2 + (W_CHUNK,) * 2 + (W_DIFF, W_FOX, W_CHUNK)
_Z_OFFS = tuple(int(c) for c in np.cumsum((0,) + _Z_SIZES))
Z_COLS = _Z_OFFS[-1]
Z_NORM_COLS = _Z_OFFS[8]
(ZB_Q1, ZB_Q2, ZB_K1, ZB_K2, ZB_QF, ZB_KF, ZB_QC, ZB_KC, ZB_VA, ZB_VF, ZB_VC) = (
    o // LANES for o in _Z_OFFS[:-1])

VMEM_LIMIT = 56 * 1024 * 1024


def _nt_dot(a, b):
    return lax.dot_general(a, b, (((1,), (1,)), ((), ())), preferred_element_type=F32)


def _dot(a, b):
    return jnp.dot(a, b, preferred_element_type=F32)


def _rms_rows(x, g):
    ms = jnp.mean(x * x, axis=-1, keepdims=True)
    return (x * lax.rsqrt(ms + EPS)) * g


def _proj_in_kernel(h_ref, g_ref, w_ref, wfg_ref, fb_ref, gain_ref, bd_ref, tri_ref,
                    z_ref, f_ref, carry_scr, *, tiles_per_batch, tn):
    i = pl.program_id(0)
    tm = h_ref.shape[0]
    hnb = _rms_rows(h_ref[...], g_ref[...]).astype(BF16)

    x = _nt_dot(wfg_ref[...], hnb) + fb_ref[:, 0:1]
    logf = jnp.minimum(x, 0.0) - jnp.log1p(jnp.exp(-jnp.abs(x)))
    hi = logf.astype(BF16)
    r1 = logf - hi.astype(F32)
    mid = r1.astype(BF16)
    lo = (r1 - mid.astype(F32)).astype(BF16)
    tri = tri_ref[...]
    cs = _dot(hi, tri) + _dot(mid, tri) + _dot(lo, tri)

    @pl.when(i % tiles_per_batch == 0)
    def _():
        carry_scr[...] = jnp.zeros_like(carry_scr)

    fcum = cs + carry_scr[:, 0:1]
    f_ref[0] = fcum * LOG2E
    carry_scr[...] = jnp.broadcast_to(fcum[:, tm - 1:tm], carry_scr.shape)

    bd = bd_ref[...]
    w = bd.shape[0]
    n_blocks = Z_COLS // tn
    project = lambda c: _dot(hnb, w_ref[:, c * tn:(c + 1) * tn])
    z_next = project(0)
    for c in range(n_blocks):
        z = z_next
        if c + 1 < n_blocks:
            z_next = project(c + 1)
        if (c + 1) * tn <= Z_NORM_COLS:
            for s in range(tn // w):
                cols = slice(c * tn + s * w, c * tn + (s + 1) * w)
                zz = z[:, s * w:(s + 1) * w]
                ss = _dot((zz * zz).astype(BF16), bd)
                inv = lax.rsqrt(ss * (1.0 / HEAD_DIM) + EPS)
                z_ref[:, cols] = (zz * inv * gain_ref[:, cols]).astype(BF16)
        else:
            z_ref[:, c * tn:(c + 1) * tn] = z.astype(BF16)


def _proj_in(h, g, w, wfg, fb, gain, bd, tri, *, batch, seq, tm, tn):
    t = h.shape[0]
    tiles_per_batch = seq // tm
    assert Z_NORM_COLS % tn == 0 and Z_COLS % tn == 0 and tn % bd.shape[0] == 0
    kern = functools.partial(_proj_in_kernel, tiles_per_batch=tiles_per_batch, tn=tn)
    const = lambda i: (0, 0)
    return pl.pallas_call(
        kern,
        grid=(t // tm,),
        in_specs=[
            pl.BlockSpec((tm, D_MODEL), lambda i: (i, 0)),
            pl.BlockSpec((1, D_MODEL), const),
            pl.BlockSpec((D_MODEL, Z_COLS), const, pipeline_mode=pl.Buffered(1)),
            pl.BlockSpec((FG_ROWS, D_MODEL), const),
            pl.BlockSpec((FG_ROWS, LANES), const),
            pl.BlockSpec((1, Z_COLS), const),
            pl.BlockSpec(bd.shape, const),
            pl.BlockSpec((tm, tm), const),
        ],
        out_specs=[
            pl.BlockSpec((tm, Z_COLS), lambda i: (i, 0)),
            pl.BlockSpec((1, FG_ROWS, tm), lambda i: (i // tiles_per_batch, 0, i % tiles_per_batch)),
        ],
        out_shape=[
            jax.ShapeDtypeStruct((t, Z_COLS), BF16),
            jax.ShapeDtypeStruct((batch, FG_ROWS, seq), F32),
        ],
        scratch_shapes=[pltpu.VMEM((FG_ROWS, LANES), F32)],
        compiler_params=pltpu.CompilerParams(
            dimension_semantics=("arbitrary",), vmem_limit_bytes=VMEM_LIMIT),
        name="proj_in",
    )(h, g, w, wfg, fb, gain, bd, tri)


def _head_lane_mask(h):
    lane = lax.broadcasted_iota(jnp.int32, (1, LANES), 1)
    return (lane >= h * HEAD_DIM) & (lane < (h + 1) * HEAD_DIM)


def _ones_lane_mask(h):
    lane = lax.broadcasted_iota(jnp.int32, (1, LANES), 1)
    return lane == ((h + 1) % HEADS_PER_BLOCK) * HEAD_DIM


def _values_with_ones(v, h):
    fill = jnp.where(_ones_lane_mask(h), 1.0, 0.0).astype(v.dtype)
    return jnp.where(_head_lane_mask(h), v, jnp.broadcast_to(fill, v.shape))


def _online_update(carry, s, vt):
    m, acc = carry
    m_new = jnp.maximum(m, jnp.max(s, axis=-1, keepdims=True))
    alpha = jnp.exp2(m - m_new)
    p = jnp.exp2(s - m_new).astype(BF16)
    return m_new, alpha * acc + _dot(p, vt)


def _init_carry(tq):
    return (jnp.full((tq, 1), NEG_INF, F32), jnp.zeros((tq, LANES), F32))


def _normalised(acc, h):
    l = jnp.sum(jnp.where(_ones_lane_mask(h), acc, 0.0), axis=-1, keepdims=True)
    return jnp.where(_head_lane_mask(h), acc / l, 0.0)


def _fox_kernel(q_ref, k_ref, v_ref, f_ref, o_ref, *, tq, tk):
    hp = pl.program_id(1)
    qi = pl.program_id(2)
    q = q_ref[0]
    row = qi * tq + lax.broadcasted_iota(jnp.int32, (tq, 1), 0)
    n_full = (qi * tq) // tk
    heads = range(HEADS_PER_BLOCK)
    qh = [jnp.where(_head_lane_mask(h), q, jnp.zeros_like(q)) for h in heads]

    def update(j, width, carry, diag=False):
        k0 = pl.multiple_of(j * width, width)
        kt = k_ref[0, pl.ds(k0, width), :]
        ss = [_nt_dot(qh[h], kt) for h in heads]
        v = v_ref[0, pl.ds(k0, width), :]
        out = []
        for h in heads:
            s = ss[h] - f_ref[0, pl.ds(hp * HEADS_PER_BLOCK + h, 1), pl.ds(k0, width)]
            if diag:
                col = k0 + lax.broadcasted_iota(jnp.int32, (1, width), 1)
                s = jnp.where(col <= row, s, NEG_INF)
            out.append(_online_update(carry[h], s, _values_with_ones(v, h)))
        return tuple(out)

    n_wide = n_full // 2
    carry = tuple(_init_carry(tq) for _ in heads)
    carry = lax.fori_loop(0, n_wide, lambda j, c: update(j, 2 * tk, c), carry)
    carry = lax.fori_loop(2 * n_wide, n_full, lambda j, c: update(j, tk, c), carry)
    carry = update(n_full, tk, carry, diag=True)
    out = _normalised(carry[0][1], 0)
    for h in heads[1:]:
        out = out + _normalised(carry[h][1], h)
    o_ref[0] = out.astype(BF16)


def _fox_attention(z, fcum, *, tq, tk):
    b, s, _ = z.shape
    nblk = W_FOX // LANES
    return pl.pallas_call(
        functools.partial(_fox_kernel, tq=tq, tk=tk),
        grid=(b, nblk, s // tq),
        in_specs=[
            pl.BlockSpec((1, tq, LANES), lambda bi, hp, qi: (bi, qi, ZB_QF + hp)),
            pl.BlockSpec((1, s, LANES), lambda bi, hp, qi: (bi, 0, ZB_KF + hp)),
            pl.BlockSpec((1, s, LANES), lambda bi, hp, qi: (bi, 0, ZB_VF + hp)),
            pl.BlockSpec((1, FG_ROWS, s), lambda bi, hp, qi: (bi, 0, 0)),
        ],
        out_specs=pl.BlockSpec((1, tq, LANES), lambda bi, hp, qi: (bi, qi, hp)),
        out_shape=jax.ShapeDtypeStruct((b, s, W_FOX), BF16),
        compiler_params=pltpu.CompilerParams(
            dimension_semantics=("parallel", "parallel", "arbitrary"), vmem_limit_bytes=VMEM_LIMIT),
        name="fox_attn",
    )(z, z, z, fcum)


def _diff_kernel(q1_ref, q2_ref, k1_ref, k2_ref, v_ref, lp_ref, sg_ref, o_ref, *, tq, tk, lam_init):
    hp = pl.program_id(1)
    qi = pl.program_id(2)
    q1 = q1_ref[0]
    q2 = q2_ref[0]
    row = qi * tq + lax.broadcasted_iota(jnp.int32, (tq, 1), 0)
    n_full = (qi * tq) // tk
    heads = range(HEADS_PER_BLOCK)

    lp = lp_ref[...]
    lam = (jnp.exp(jnp.sum(lp[0:1] * lp[1:2], axis=-1, keepdims=True))
           - jnp.exp(jnp.sum(lp[2:3] * lp[3:4], axis=-1, keepdims=True)) + lam_init)

    q1h = [jnp.where(_head_lane_mask(h), q1, jnp.zeros_like(q1)) for h in heads]
    q2h = [jnp.where(_head_lane_mask(h), q2, jnp.zeros_like(q2)) for h in heads]
    slope = [jnp.where(hp == 0, ALIBI_SLOPES[h] * LOG2E, ALIBI_SLOPES[HEADS_PER_BLOCK + h] * LOG2E)
             .astype(F32) for h in heads]

    def tile(j, width, carry, diag=False):
        k0 = pl.multiple_of(j * width, width)
        k1t = k1_ref[0, pl.ds(k0, width), :]
        k2t = k2_ref[0, pl.ds(k0, width), :]
        v = v_ref[0, pl.ds(k0, width), :]
        col = k0 + lax.broadcasted_iota(jnp.int32, (1, width), 1)
        ss1 = [_nt_dot(q1h[h], k1t) for h in heads]
        ss2 = [_nt_dot(q2h[h], k2t) for h in heads]
        out = []
        for h in heads:
            vt = _values_with_ones(v, h)
            s1 = ss1[h]
            s2 = ss2[h]
            if diag:
                bias = slope[h] * jnp.minimum(col, 2 * row - col).astype(F32)
                allowed = (col // CHUNK) <= (row // CHUNK)
                s1 = jnp.where(allowed, s1 + bias, NEG_INF)
                s2 = jnp.where(allowed, s2 + bias, NEG_INF)
            else:
                bias = slope[h] * col.astype(F32)
                s1 = s1 + bias
                s2 = s2 + bias
            c1, c2 = carry[h]
            out.append((_online_update(c1, s1, vt), _online_update(c2, s2, vt)))
        return tuple(out)

    n_wide = n_full // 2
    carry = tuple((_init_carry(tq), _init_carry(tq)) for _ in heads)
    carry = lax.fori_loop(0, n_wide, lambda j, c: tile(j, 2 * tk, c), carry)
    carry = lax.fori_loop(2 * n_wide, n_full, lambda j, c: tile(j, tk, c), carry)
    carry = tile(n_full, tk, carry, diag=True)
    out = jnp.zeros((tq, LANES), F32)
    for h in heads:
        (_, a1), (_, a2) = carry[h]
        out = out + (_normalised(a1, h) - lam * _normalised(a2, h))

    sq = out * out
    inv = jnp.zeros((tq, LANES), F32)
    for h in heads:
        hm = _head_lane_mask(h)
        ms = jnp.sum(jnp.where(hm, sq, 0.0), axis=-1, keepdims=True) * (1.0 / HEAD_DIM)
        inv = jnp.where(hm, lax.rsqrt(ms + EPS), inv)
    o_ref[0] = ((out * inv) * sg_ref[...] * (1.0 - lam_init)).astype(BF16)


def _diff_attention(z, lam_params, subln2, *, tq, tk, lam_init):
    b, s, _ = z.shape
    nblk = W_DIFF // LANES
    return pl.pallas_call(
        functools.partial(_diff_kernel, tq=tq, tk=tk, lam_init=lam_init),
        grid=(b, nblk, s // tq),
        in_specs=[
            pl.BlockSpec((1, tq, LANES), lambda bi, hp, qi: (bi, qi, ZB_Q1 + hp)),
            pl.BlockSpec((1, tq, LANES), lambda bi, hp, qi: (bi, qi, ZB_Q2 + hp)),
            pl.BlockSpec((1, s, LANES), lambda bi, hp, qi: (bi, 0, ZB_K1 + hp)),
            pl.BlockSpec((1, s, LANES), lambda bi, hp, qi: (bi, 0, ZB_K2 + hp)),
            pl.BlockSpec((1, s, LANES), lambda bi, hp, qi: (bi, 0, ZB_VA + hp)),
            pl.BlockSpec((4, HEAD_DIM), lambda bi, hp, qi: (0, 0)),
            pl.BlockSpec((1, LANES), lambda bi, hp, qi: (0, 0)),
        ],
        out_specs=pl.BlockSpec((1, tq, LANES), lambda bi, hp, qi: (bi, qi, hp)),
        out_shape=jax.ShapeDtypeStruct((b, s, W_DIFF), BF16),
        compiler_params=pltpu.CompilerParams(
            dimension_semantics=("parallel", "parallel", "arbitrary"), vmem_limit_bytes=VMEM_LIMIT),
        name="diff_attn",
    )(z, z, z, z, z, lam_params, subln2)


CHUNK_TILE = 256
CHUNK_NTILES = 1 + (N_PREV_CHUNKS * CHUNK) // CHUNK_TILE
CHUNK_QTILES = 8


def _chunk_kernel(q_ref, k_ref, v_ref, bm_ref, o_ref):
    t = CHUNK_TILE
    heads = range(HEADS_PER_BLOCK)
    subs = range(CHUNK_QTILES)
    tiles = range(CHUNK_NTILES)
    qts = [pl.program_id(2) * CHUNK_QTILES + u for u in subs]
    k0s = [[pl.multiple_of(jnp.maximum(qts[u] - d, 0) * t, t) for d in tiles] for u in subs]
    raw = []
    for u in subs:
        q = q_ref[0, u * t:(u + 1) * t, :]
        qh = [jnp.where(_head_lane_mask(h), q, jnp.zeros_like(q)) for h in heads]
        raw.append([[_nt_dot(qh[h], k_ref[0, pl.ds(k0s[u][d], t), :]) for d in tiles] for h in heads])
    for u in subs:
        out = jnp.zeros((t, LANES), F32)
        for h in heads:
            ss = []
            vs = []
            for d in tiles:
                s = raw[u][h][d] + bm_ref[h, d]
                ss.append(jnp.where(qts[u] - d >= 0, s, NEG_INF))
                vs.append(_values_with_ones(v_ref[0, pl.ds(k0s[u][d], t), :], h))
            m = ss[0].max(axis=-1, keepdims=True)
            for s in ss[1:]:
                m = jnp.maximum(m, s.max(axis=-1, keepdims=True))
            acc = jnp.zeros((t, LANES), F32)
            for s, vt in zip(ss, vs):
                acc = acc + _dot(jnp.exp2(s - m).astype(BF16), vt)
            out = out + _normalised(acc, h)
        o_ref[0, u * t:(u + 1) * t, :] = out.astype(BF16)


def _chunk_attention(z, bm):
    b, s, _ = z.shape
    t = CHUNK_TILE
    tq = CHUNK_TILE * CHUNK_QTILES
    nblk = W_CHUNK // LANES
    return pl.pallas_call(
        _chunk_kernel,
        grid=(b, nblk, s // tq),
        in_specs=[
            pl.BlockSpec((1, tq, LANES), lambda bi, hp, qi: (bi, qi, ZB_QC + hp)),
            pl.BlockSpec((1, s, LANES), lambda bi, hp, qi: (bi, 0, ZB_KC + hp)),
            pl.BlockSpec((1, s, LANES), lambda bi, hp, qi: (bi, 0, ZB_VC + hp)),
            pl.BlockSpec((HEADS_PER_BLOCK, CHUNK_NTILES, t, t), lambda bi, hp, qi: (hp, 0, 0, 0)),
        ],
        out_specs=pl.BlockSpec((1, tq, LANES), lambda bi, hp, qi: (bi, qi, hp)),
        out_shape=jax.ShapeDtypeStruct((b, s, W_CHUNK), BF16),
        compiler_params=pltpu.CompilerParams(
            dimension_semantics=("parallel", "parallel", "arbitrary"), vmem_limit_bytes=VMEM_LIMIT),
        name="chunk_attn",
    )(z, z, z, bm)


def _chunk_bias_table(rel_table):
    t = CHUNK_TILE
    period = 2 * t
    u = np.arange(period)
    rel = np.where(u < t, -u, period - u)
    idx = np.stack([np.clip(rel + t * d, -REL_CLIP, REL_CLIP) + REL_CLIP for d in range(CHUNK_NTILES)])
    vec = rel_table.astype(F32)[:, idx] * LOG2E
    flat = jnp.tile(vec, (1, 1, t))[..., :t * (period - 1)]
    toep = flat.reshape(vec.shape[0], CHUNK_NTILES, t, period - 1)[..., :t]
    a = np.arange(t)[:, None]
    b = np.arange(t)[None, :]
    valid = []
    for d in range(CHUNK_NTILES):
        qc = a // CHUNK + (t // CHUNK) * d
        kc = b // CHUNK
        valid.append((kc <= qc) & (kc >= qc - N_PREV_CHUNKS))
    return jnp.where(np.stack(valid)[None], toep, NEG_INF)


MLP_CW = 256
HALO = 8
MLP_LEAD = 3


def _channel_kernel(h_ref, oa_ref, ob_ref, oc_ref, wo_ref, g_ref, wup_ref, cw_ref, cb_ref, wdn_ref,
                    p_ref, gp_ref, wg_ref, wp_ref, o_ref,
                    cat_scr, hn_scr, ubuf, tail, res, acc, *, tiles_per_batch):
    i = pl.program_id(0)
    tm = h_ref.shape[0]

    @pl.when(i % tiles_per_batch == 0)
    def _():
        tail[...] = jnp.zeros_like(tail)

    cat_scr[:, 0:W_DIFF] = oa_ref[...]
    cat_scr[:, W_DIFF:W_DIFF + W_FOX] = ob_ref[...]
    cat_scr[:, W_DIFF + W_FOX:MIX_WIDTH] = oc_ref[...]
    res[...] = h_ref[...] + _dot(cat_scr[...], wo_ref[...])
    hn_scr[...] = _rms_rows(res[...], g_ref[...]).astype(BF16)

    def up(c):
        g0 = c * MLP_CW
        v0 = D_FF + c * MLP_CW
        return (_dot(hn_scr[...], wup_ref[:, g0:g0 + MLP_CW]), _dot(hn_scr[...], wup_ref[:, v0:v0 + MLP_CW]))

    def conv(u, c0, slot):
        cols = slice(c0, c0 + MLP_CW)
        ubuf[slot, 0:HALO, :] = tail[:, cols]
        ubuf[slot, HALO:HALO + tm, :] = u
        tail[:, cols] = u[tm - HALO:tm, :]
        return (cb_ref[:, cols]
                + cw_ref[0:1, cols] * ubuf[slot, HALO - 2:HALO - 2 + tm, :]
                + cw_ref[1:2, cols] * ubuf[slot, HALO - 1:HALO - 1 + tm, :]
                + cw_ref[2:3, cols] * u)

    n_chunks = D_FF // MLP_CW
    ahead = [up(c) for c in range(min(MLP_LEAD, n_chunks))]
    for c in range(n_chunks):
        ug, uv = ahead.pop(0)
        if c + MLP_LEAD < n_chunks:
            ahead.append(up(c + MLP_LEAD))
        gate = conv(ug, c * MLP_CW, 0)
        val = conv(uv, D_FF + c * MLP_CW, 1)
        act = (gate * jax.nn.sigmoid(gate) * val).astype(BF16)
        part = _dot(act, wdn_ref[c * MLP_CW:(c + 1) * MLP_CW, :])
        if c == 0:
            acc[...] = part
        else:
            acc[...] += part

    h2 = res[...] + acc[...]
    hn2 = _rms_rows(h2, gp_ref[...]).astype(BF16)
    gate = jax.nn.sigmoid(_dot(hn2, wg_ref[...]))
    proj = _dot(p_ref[...].astype(BF16), wp_ref[...])
    o_ref[...] = h2 + proj * gate


def _channel_mix(h, oa, ob, oc, wo, g, wup, cw, cb, wdn, p, gp, wg, wp, *, seq, tm):
    t = h.shape[0]
    const = lambda i: (0, 0)
    rows = lambda width: pl.BlockSpec((tm, width), lambda i: (i, 0))
    resident = lambda shape: pl.BlockSpec(shape, const, pipeline_mode=pl.Buffered(1))
    return pl.pallas_call(
        functools.partial(_channel_kernel, tiles_per_batch=seq // tm),
        grid=(t // tm,),
        in_specs=[
            rows(D_MODEL), rows(W_DIFF), rows(W_FOX), rows(W_CHUNK),
            resident((MIX_WIDTH, D_MODEL)),
            pl.BlockSpec((1, D_MODEL), const),
            resident((D_MODEL, 2 * D_FF)),
            pl.BlockSpec((CONV_WIDTH, 2 * D_FF), const),
            pl.BlockSpec((1, 2 * D_FF), const),
            resident((D_FF, D_MODEL)),
            rows(PLE_DIM),
            pl.BlockSpec((1, D_MODEL), const),
            resident((D_MODEL, D_MODEL)),
            resident((PLE_DIM, D_MODEL)),
        ],
        out_specs=rows(D_MODEL),
        out_shape=jax.ShapeDtypeStruct((t, D_MODEL), F32),
        scratch_shapes=[
            pltpu.VMEM((tm, MIX_WIDTH), BF16),
            pltpu.VMEM((tm, D_MODEL), BF16),
            pltpu.VMEM((2, HALO + tm, MLP_CW), F32),
            pltpu.VMEM((HALO, 2 * D_FF), F32),
            pltpu.VMEM((tm, D_MODEL), F32),
            pltpu.VMEM((tm, D_MODEL), F32),
        ],
        compiler_params=pltpu.CompilerParams(
            dimension_semantics=("arbitrary",), vmem_limit_bytes=VMEM_LIMIT),
        name="channel_mix",
    )(h, oa, ob, oc, wo, g, wup, cw, cb, wdn, p, gp, wg, wp)


def _prep_w_in(w_in, qk_gain):
    sizes = (W_DIFF,) * 5 + (W_FOX,) * 3 + (H_FOX,) + (W_CHUNK,) * 3
    offs = np.cumsum((0,) + sizes)
    seg = lambda n: w_in[:, offs[n]:offs[n + 1]]
    q1, q2, k1, k2, va, qf, kf, vf, fg, qc, kc, vc = (seg(n) for n in range(12))
    w = jnp.concatenate([q1, q2, k1, k2, qf, kf, qc, kc, va, vf, vc], axis=1).astype(BF16)
    wfg = jnp.zeros((FG_ROWS, D_MODEL), F32).at[:H_FOX].set(fg.T).astype(BF16)
    scale = HEAD_DIM ** -0.5 * LOG2E
    g = qk_gain.astype(F32)
    gains = jnp.concatenate([
        jnp.tile(g[0] * scale, H_DIFF), jnp.tile(g[0] * scale, H_DIFF),
        jnp.tile(g[1], H_DIFF), jnp.tile(g[1], H_DIFF),
        jnp.tile(g[2] * scale, H_FOX), jnp.tile(g[3], H_FOX),
        jnp.tile(g[4] * scale, H_CHUNK), jnp.tile(g[5], H_CHUNK),
        jnp.ones((Z_COLS - Z_NORM_COLS,), F32)])[None, :]
    return w, wfg, gains


def _block_diag_ones(width):
    r = np.arange(width)
    return jnp.asarray((r[:, None] // HEAD_DIM) == (r[None, :] // HEAD_DIM), BF16)


def _pick(n, pref):
    while n % pref:
        pref //= 2
    return pref


def kernel(x, p, ln_mix, w_in, qk_gain, lam_params, subln_gain, fgate_bias, rel_bias, w_out,
           ln_ffn, w_up, conv_w, conv_b, w_down, ln_ple, w_ple_gate, w_ple_proj):
    b, s, d = x.shape
    depth = w_in.shape[0]
    t = b * s
    tm_proj = _pick(s, 512)
    tn_proj = 512
    tm_mlp = _pick(s, 512)
    tq = _pick(s, 512)
    tk = tq

    bd = _block_diag_ones(256)
    tri = jnp.asarray(np.triu(np.ones((tm_proj, tm_proj), np.float32)), BF16)

    h = x.reshape(t, d).astype(F32)
    for i in range(depth):
        lam_init = 0.8 - 0.6 * math.exp(-0.3 * i)
        w, wfg, gains = _prep_w_in(w_in[i], qk_gain[i])
        fb = jnp.zeros((FG_ROWS, LANES), F32).at[:H_FOX, :].set(
            jnp.broadcast_to(fgate_bias[i].astype(F32)[:, None], (H_FOX, LANES)))
        z, fcum = _proj_in(h, ln_mix[i][None, :].astype(F32), w, wfg, fb, gains, bd, tri,
                           batch=b, seq=s, tm=tm_proj, tn=tn_proj)
        z = z.reshape(b, s, Z_COLS)
        subln2 = jnp.tile(subln_gain[i].astype(F32), HEADS_PER_BLOCK)[None, :]
        oa = _diff_attention(z, lam_params[i].astype(F32), subln2, tq=tq, tk=tk, lam_init=lam_init)
        ob = _fox_attention(z, fcum, tq=tq, tk=tk)
        oc = _chunk_attention(z, _chunk_bias_table(rel_bias[i]))
        h = _channel_mix(
            h, oa.reshape(t, W_DIFF), ob.reshape(t, W_FOX), oc.reshape(t, W_CHUNK), w_out[i].astype(BF16),
            ln_ffn[i][None, :].astype(F32), w_up[i].astype(BF16), conv_w[i].astype(F32),
            conv_b[i][None, :].astype(F32), w_down[i].astype(BF16),
            p[i].reshape(t, PLE_DIM), ln_ple[i][None, :].astype(F32),
            w_ple_gate[i].astype(BF16), w_ple_proj[i].astype(BF16), seq=s, tm=tm_mlp)
    return h.reshape(b, s, d).astype(x.dtype)
```
